```python
import math
import jax, jax.numpy as jnp
from jax import lax
import numpy as np

D_MODEL = 1024
BATCH = 2
SEQ = 8192
DEPTH = 2
DEC_BATCH = 16
DEC_SEQ = 64
PAST_LEN = 2048

CHUNK = 64
GLA_HEADS = 4
GLA_DK = 32
GLA_DV = 64
GLA_RANK = 16
GLA_TAU = 16.0
S5_GROUPS = 16
S5_CH = 16
S5_STATE = 64
S5_DT_MIN = 0.001
S5_DT_MAX = 0.1
FOX_HEADS = 8
FOX_HD = 64
FOX_BIAS_INIT = 4.0
Q_BLOCK = 128
MEM_TOKENS = 256
MEM_HEADS = 4
MEM_HD = 256
D_FF = 2816
CONV_W = 3
LN_EPS = 1e-5
DN_ALPHA = (2 * DEPTH) ** 0.25
DN_BETA = (8 * DEPTH) ** -0.25

GLA_W = GLA_HEADS * GLA_DV
S5_W = S5_GROUPS * S5_CH
FOX_W = FOX_HEADS * FOX_HD
MIX_W = GLA_W + S5_W + FOX_W
IN_SPLITS = (GLA_HEADS * GLA_DK, GLA_HEADS * GLA_DK, GLA_W, GLA_W, GLA_RANK, S5_W, FOX_W, FOX_W, FOX_W, FOX_HEADS)
IN_COL_SCALE = (1.0, 1.0, DN_BETA, 1.0, 1.0, 1.0, 1.0, 1.0, DN_BETA, 1.0)
N_IN = sum(IN_SPLITS)

kernel_name = 'hybrid_gla_s5_fox_streaming_encoder_step'


def _layer_norm(x, g, b):
    xf = x.astype(jnp.float32)
    mu = jnp.mean(xf, axis=-1, keepdims=True)
    var = jnp.mean(jnp.square(xf - mu), axis=-1, keepdims=True)
    y = (xf - mu) * lax.rsqrt(var + LN_EPS) * g.astype(jnp.float32) + b.astype(jnp.float32)
    return y.astype(x.dtype)


def _gla_mix(q, k, v, g_out, a_lr, w_a2, b_a, norm_g, s0):
    f32 = jnp.float32
    bn, t, _ = q.shape
    q = q.reshape(bn, t, GLA_HEADS, GLA_DK).astype(f32) * (GLA_DK ** -0.5)
    k = k.reshape(bn, t, GLA_HEADS, GLA_DK).astype(f32)
    v = v.reshape(bn, t, GLA_HEADS, GLA_DV).astype(f32)
    log_a = jax.nn.log_sigmoid((a_lr @ w_a2 + b_a).astype(f32)) / GLA_TAU
    log_a = log_a.reshape(bn, t, GLA_HEADS, GLA_DK)
    pad = (-t) % CHUNK
    nc = (t + pad) // CHUNK

    def blocks(a):
        a = jnp.pad(a, ((0, 0), (0, pad), (0, 0), (0, 0)))
        return a.reshape(bn, nc, CHUNK, a.shape[2], a.shape[3])

    qc, kc, vc, gc = blocks(q), blocks(k), blocks(v), blocks(log_a)
    cum = jnp.cumsum(gc, axis=2)
    cum_end = cum[:, :, -1:]
    q_dec = qc * jnp.exp(cum)
    k_inv = kc * jnp.exp(-cum)
    k_end = kc * jnp.exp(cum_end - cum)
    causal = jnp.tril(jnp.ones((CHUNK, CHUNK), dtype=bool))
    scores = jnp.where(causal, jnp.einsum('bclhd,bcshd->bchls', q_dec, k_inv), 0.0)
    o = jnp.einsum('bchls,bcshe->bclhe', scores, vc)
    ds = jnp.einsum('bclhd,bclhe->bchde', k_end, vc)
    decay = jnp.exp(cum_end[:, :, 0])

    def step(s, inp):
        dec, d_s = inp
        return dec[..., None] * s + d_s, s

    s_fin, s_prev = lax.scan(step, s0.astype(f32), (jnp.moveaxis(decay, 1, 0), jnp.moveaxis(ds, 1, 0)))
    o = o + jnp.einsum('bclhd,cbhde->bclhe', q_dec, s_prev)
    o = o.reshape(bn, nc * CHUNK, GLA_HEADS, GLA_DV)[:, :t]
    o = o * lax.rsqrt(jnp.mean(jnp.square(o), axis=-1, keepdims=True) + LN_EPS) * norm_g.astype(f32)
    out = o.reshape(bn, t, GLA_W) * jax.nn.silu(g_out.astype(f32))
    return out.astype(g_out.dtype), s_fin


def _complex_affine_combine(e1, e2):
    a1r, a1i, b1r, b1i = e1
    a2r, a2i, b2r, b2i = e2
    return (a2r * a1r - a2i * a1i,
            a2r * a1i + a2i * a1r,
            a2r * b1r - a2i * b1i + b2r,
            a2r * b1i + a2i * b1r + b2i)


def _s5_mix(u, lam_re, lam_im, log_dt, b_re, b_im, c_re, c_im, d_skip, w_glu, b_glu, h0_re, h0_im):
    f32 = jnp.float32
    bn, t, _ = u.shape
    lam_re, lam_im = lam_re.astype(f32), lam_im.astype(f32)
    dt = jnp.exp(log_dt.astype(f32))
    mag = jnp.exp(lam_re * dt)
    ab_re = mag * jnp.cos(lam_im * dt)
    ab_im = mag * jnp.sin(lam_im * dt)
    den = jnp.square(lam_re) + jnp.square(lam_im)
    f_re = ((ab_re - 1.0) * lam_re + ab_im * lam_im) / den
    f_im = (ab_im * lam_re - (ab_re - 1.0) * lam_im) / den
    b_re, b_im = b_re.astype(f32), b_im.astype(f32)
    bb_re = f_re[..., None] * b_re - f_im[..., None] * b_im
    bb_im = f_re[..., None] * b_im + f_im[..., None] * b_re
    uf = u.astype(f32).reshape(bn, t, S5_GROUPS, S5_CH)
    bu_re = jnp.einsum('gph,btgh->btgp', bb_re, uf)
    bu_im = jnp.einsum('gph,btgh->btgp', bb_im, uf)
    a_re = jnp.broadcast_to(ab_re, bu_re.shape)
    a_im = jnp.broadcast_to(ab_im, bu_im.shape)
    pr, pi, xr, xi = lax.associative_scan(_complex_affine_combine, (a_re, a_im, bu_re, bu_im), axis=1)
    h0r = h0_re.astype(f32)[:, None]
    h0i = h0_im.astype(f32)[:, None]
    xr = xr + pr * h0r - pi * h0i
    xi = xi + pr * h0i + pi * h0r
    y = jnp.einsum('ghp,btgp->btgh', c_re.astype(f32), xr) - jnp.einsum('ghp,btgp->btgh', c_im.astype(f32), xi)
    y = y.reshape(bn, t, S5_W) + d_skip.astype(f32) * uf.reshape(bn, t, S5_W)
    z = jax.nn.gelu(y)
    out = z * jax.nn.sigmoid(z @ w_glu.astype(f32) + b_glu.astype(f32))
    return out.astype(u.dtype), xr[:, -1], xi[:, -1]


def _fox_attend(q, k, v, cf_q, cf_k, q_pos, k_pos):
    bn, tq = q.shape[0], q.shape[1]
    blk = Q_BLOCK if tq % Q_BLOCK == 0 else tq
    nb = tq // blk
    qb = q.reshape(bn, nb, blk, FOX_HEADS, FOX_HD).swapaxes(0, 1)
    fqb = cf_q.reshape(bn, nb, blk, FOX_HEADS).swapaxes(0, 1)
    pb = q_pos.reshape(nb, blk)
    fk_t = cf_k.transpose(0, 2, 1)[:, :, None, :]

    def one_block(args):
        qi, fi, pi = args
        s = jnp.einsum('bqhd,bkhd->bhqk', qi, k, preferred_element_type=jnp.float32) * (FOX_HD ** -0.5)
        s = s + fi.transpose(0, 2, 1)[..., None] - fk_t
        s = jnp.where(k_pos[None, :] <= pi[:, None], s, -jnp.inf)
        p = jax.nn.softmax(s, axis=-1)
        return jnp.einsum('bhqk,bkhd->bqhd', p.astype(v.dtype), v)

    o = lax.map(one_block, (qb, fqb, pb))
    return o.swapaxes(0, 1).reshape(bn, tq, FOX_W)


def _mem_attend(x, mem_k, mem_v, w_q, w_o):
    bn, t, _ = x.shape
    q = (x @ w_q).reshape(bn, t, MEM_HEADS, MEM_HD)
    s = jnp.einsum('bthd,bmhd->bhtm', q, mem_k.astype(q.dtype), preferred_element_type=jnp.float32) * (MEM_HD ** -0.5)
    p = jax.nn.softmax(s, axis=-1)
    o = jnp.einsum('bhtm,bmhd->bthd', p.astype(x.dtype), mem_v.astype(x.dtype))
    return o.reshape(bn, t, MEM_HEADS * MEM_HD) @ w_o


def _conv_ffn(x, w_up, conv_w, conv_b, w_down, prev):
    t = x.shape[1]
    u = x @ w_up
    ext = jnp.concatenate([prev.astype(u.dtype), u], axis=1)
    y = conv_b
    for j in range(CONV_W):
        y = y + conv_w[j] * ext[:, j:j + t]
    a, g = jnp.split(y, 2, axis=-1)
    h = jax.nn.gelu(a) * g
    return h @ w_down, ext[:, t:]


def _hybrid_layer(x, mem_k, mem_v, gla_s0, s5_h0_re, s5_h0_im, past_k, past_v, past_logf, conv_prev, p):
    f32 = jnp.float32
    bn, t, _ = x.shape
    proj = x @ p['w_in']
    offsets = np.cumsum(IN_SPLITS)[:-1].tolist()
    g_q, g_k, g_v, g_o, g_a, s_u, f_q, f_k, f_v, f_f = jnp.split(proj, offsets, axis=-1)
    o_gla, gla_s = _gla_mix(g_q, g_k, g_v, g_o, g_a, p['gla_w_a2'], p['gla_b_a'], p['gla_norm_g'], gla_s0)
    o_s5, s5_re, s5_im = _s5_mix(s_u, p['s5_lam_re'], p['s5_lam_im'], p['s5_log_dt'], p['s5_b_re'], p['s5_b_im'],
                                 p['s5_c_re'], p['s5_c_im'], p['s5_d'], p['s5_w_glu'], p['s5_b_glu'], s5_h0_re, s5_h0_im)
    q = f_q.reshape(bn, t, FOX_HEADS, FOX_HD)
    k_new = f_k.reshape(bn, t, FOX_HEADS, FOX_HD)
    v_new = f_v.reshape(bn, t, FOX_HEADS, FOX_HD)
    logf = jax.nn.log_sigmoid((f_f + p['fox_b_f']).astype(f32))
    if past_k is None:
        k_all, v_all, logf_all = k_new, v_new, logf
    else:
        k_all = jnp.concatenate([past_k.astype(k_new.dtype), k_new], axis=1)
        v_all = jnp.concatenate([past_v.astype(v_new.dtype), v_new], axis=1)
        logf_all = jnp.concatenate([past_logf.astype(f32), logf], axis=1)
    past = k_all.shape[1] - t
    cum_f = jnp.cumsum(logf_all, axis=1)
    o_fox = _fox_attend(q, k_all, v_all, cum_f[:, past:], cum_f, past + jnp.arange(t), jnp.arange(past + t))
    mixed = jnp.concatenate([o_gla, o_s5, o_fox.astype(x.dtype)], axis=-1) @ p['w_mix_out']
    x = _layer_norm(DN_ALPHA * x + mixed, p['ln1_g'], p['ln1_b'])
    x = _layer_norm(DN_ALPHA * x + _mem_attend(x, mem_k, mem_v, p['mem_w_q'], p['mem_w_o']), p['ln2_g'], p['ln2_b'])
    f_out, conv_new = _conv_ffn(x, p['ffn_w_up'], p['ffn_conv_w'], p['ffn_conv_b'], p['ffn_w_down'], conv_prev)
    x = _layer_norm(DN_ALPHA * x + f_out, p['ln3_g'], p['ln3_b'])
    return x, (gla_s, s5_re, s5_im, k_new, v_new, logf, conv_new)


def setup_inputs(seed: int = 0) -> dict:
    key = jax.random.key(seed)
    keys = jax.random.split(key, 64)
    cnt = [0]
    f32 = jnp.float32

    def nk():
        kk = keys[cnt[0]]
        cnt[0] += 1
        return kk

    def nrm(shape, scale=1.0):
        return jax.random.normal(nk(), shape, f32) * scale

    L = DEPTH
    col_scale = jnp.concatenate([jnp.full((n,), s, f32) for n, s in zip(IN_SPLITS, IN_COL_SCALE)])
    return {
        'x_prompt': nrm((BATCH, SEQ, D_MODEL)),
        'x_sample': nrm((DEC_BATCH, DEC_SEQ, D_MODEL)),
        'mem_prompt': nrm((BATCH, MEM_TOKENS, D_MODEL)),
        'state_gla': nrm((L, DEC_BATCH, GLA_HEADS, GLA_DK, GLA_DV), 0.3),
        'state_s5_re': nrm((L, DEC_BATCH, S5_GROUPS, S5_STATE), 0.1),
        'state_s5_im': nrm((L, DEC_BATCH, S5_GROUPS, S5_STATE), 0.1),
        'cache_fox_k': nrm((L, DEC_BATCH, PAST_LEN, FOX_HEADS, FOX_HD)),
        'cache_fox_v': nrm((L, DEC_BATCH, PAST_LEN, FOX_HEADS, FOX_HD), DN_BETA),
        'cache_fox_logf': jax.nn.log_sigmoid(FOX_BIAS_INIT + nrm((L, DEC_BATCH, PAST_LEN, FOX_HEADS))),
        'cache_mem_k': nrm((L, DEC_BATCH, MEM_TOKENS, MEM_HEADS, MEM_HD)),
        'cache_mem_v': nrm((L, DEC_BATCH, MEM_TOKENS, MEM_HEADS, MEM_HD), DN_BETA),
        'state_ffn_conv': nrm((L, DEC_BATCH, CONV_W - 1, 2 * D_FF), DN_BETA),
        'ln_in_g': 1.0 + nrm((D_MODEL,), 0.02),
        'ln_in_b': nrm((D_MODEL,), 0.02),
        'w_in': nrm((L, D_MODEL, N_IN), D_MODEL ** -0.5) * col_scale,
        'gla_w_a2': nrm((L, GLA_RANK, GLA_HEADS * GLA_DK), GLA_RANK ** -0.5),
        'gla_b_a': nrm((L, GLA_HEADS * GLA_DK), 0.1),
        'gla_norm_g': 1.0 + nrm((L, GLA_DV), 0.02),
        's5_lam_re': -0.5 + nrm((L, S5_GROUPS, S5_STATE), 0.01),
        's5_lam_im': math.pi * jnp.arange(S5_STATE, dtype=f32) + nrm((L, S5_GROUPS, S5_STATE), 0.01),
        's5_log_dt': jax.random.uniform(nk(), (L, S5_GROUPS, S5_STATE), f32, math.log(S5_DT_MIN), math.log(S5_DT_MAX)),
        's5_b_re': nrm((L, S5_GROUPS, S5_STATE, S5_CH), (2 * S5_CH) ** -0.5),
        's5_b_im': nrm((L, S5_GROUPS, S5_STATE, S5_CH), (2 * S5_CH) ** -0.5),
        's5_c_re': nrm((L, S5_GROUPS, S5_CH, S5_STATE), S5_STATE ** -0.5),
        's5_c_im': nrm((L, S5_GROUPS, S5_CH, S5_STATE), S5_STATE ** -0.5),
        's5_d': nrm((L, S5_W)),
        's5_w_glu': nrm((L, S5_W, S5_W), S5_W ** -0.5),
        's5_b_glu': nrm((L, S5_W), 0.02),
        'fox_b_f': FOX_BIAS_INIT + nrm((L, FOX_HEADS), 0.1),
        'w_mix_out': nrm((L, MIX_W, D_MODEL), MIX_W ** -0.5 * DN_BETA),
        'ln1_g': 1.0 + nrm((L, D_MODEL), 0.02),
        'ln1_b': nrm((L, D_MODEL), 0.02),
        'mem_w_q': nrm((L, D_MODEL, MEM_HEADS * MEM_HD), D_MODEL ** -0.5),
        'mem_w_k': nrm((L, D_MODEL, MEM_HEADS * MEM_HD), D_MODEL ** -0.5),
        'mem_w_v': nrm((L, D_MODEL, MEM_HEADS * MEM_HD), D_MODEL ** -0.5 * DN_BETA),
        'mem_w_o': nrm((L, MEM_HEADS * MEM_HD, D_MODEL), (MEM_HEADS * MEM_HD) ** -0.5 * DN_BETA),
        'ln2_g': 1.0 + nrm((L, D_MODEL), 0.02),
        'ln2_b': nrm((L, D_MODEL), 0.02),
        'ffn_w_up': nrm((L, D_MODEL, 2 * D_FF), D_MODEL ** -0.5 * DN_BETA),
        'ffn_conv_w': nrm((L, CONV_W, 2 * D_FF), CONV_W ** -0.5),
        'ffn_conv_b': nrm((L, 2 * D_FF), 0.02),
        'ffn_w_down': nrm((L, D_FF, D_MODEL), D_FF ** -0.5 * DN_BETA),
        'ln3_g': 1.0 + nrm((L, D_MODEL), 0.02),
        'ln3_b': nrm((L, D_MODEL), 0.02),
    }


def reference(x_prompt, x_sample, mem_prompt, state_gla, state_s5_re, state_s5_im, cache_fox_k, cache_fox_v,
              cache_fox_logf, cache_mem_k, cache_mem_v, state_ffn_conv, ln_in_g, ln_in_b, w_in, gla_w_a2, gla_b_a,
              gla_norm_g, s5_lam_re, s5_lam_im, s5_log_dt, s5_b_re, s5_b_im, s5_c_re, s5_c_im, s5_d, s5_w_glu,
              s5_b_glu, fox_b_f, w_mix_out, ln1_g, ln1_b, mem_w_q, mem_w_k, mem_w_v, mem_w_o, ln2_g, ln2_b,
              ffn_w_up, ffn_conv_w, ffn_conv_b, ffn_w_down, ln3_g, ln3_b):
    f32 = jnp.float32
    bp = x_prompt.shape[0]
    hp = _layer_norm(x_prompt, ln_in_g, ln_in_b)
    hs = _layer_norm(x_sample, ln_in_g, ln_in_b)
    zero_gla = jnp.zeros((bp, GLA_HEADS, GLA_DK, GLA_DV), f32)
    zero_s5 = jnp.zeros((bp, S5_GROUPS, S5_STATE), f32)
    zero_conv = jnp.zeros((bp, CONV_W - 1, 2 * D_FF), x_prompt.dtype)
    prompt_states = []
    sample_states = []
    for l in range(DEPTH):
        prm = dict(w_in=w_in[l], gla_w_a2=gla_w_a2[l], gla_b_a=gla_b_a[l], gla_norm_g=gla_norm_g[l],
                   s5_lam_re=s5_lam_re[l], s5_lam_im=s5_lam_im[l], s5_log_dt=s5_log_dt[l], s5_b_re=s5_b_re[l],
                   s5_b_im=s5_b_im[l], s5_c_re=s5_c_re[l], s5_c_im=s5_c_im[l], s5_d=s5_d[l], s5_w_glu=s5_w_glu[l],
                   s5_b_glu=s5_b_glu[l], fox_b_f=fox_b_f[l], w_mix_out=w_mix_out[l], ln1_g=ln1_g[l], ln1_b=ln1_b[l],
                   mem_w_q=mem_w_q[l], mem_w_o=mem_w_o[l], ln2_g=ln2_g[l], ln2_b=ln2_b[l], ffn_w_up=ffn_w_up[l],
                   ffn_conv_w=ffn_conv_w[l], ffn_conv_b=ffn_conv_b[l], ffn_w_down=ffn_w_down[l],
                   ln3_g=ln3_g[l], ln3_b=ln3_b[l])
        mk_p = (mem_prompt @ mem_w_k[l]).reshape(bp, MEM_TOKENS, MEM_HEADS, MEM_HD)
        mv_p = (mem_prompt @ mem_w_v[l]).reshape(bp, MEM_TOKENS, MEM_HEADS, MEM_HD)
        hp, st_p = _hybrid_layer(hp, mk_p, mv_p, zero_gla, zero_s5, zero_s5, None, None, None, zero_conv, prm)
        prompt_states.append(st_p + (mk_p, mv_p))
        hs, st_s = _hybrid_layer(hs, cache_mem_k[l], cache_mem_v[l], state_gla[l], state_s5_re[l], state_s5_im[l],
                                 cache_fox_k[l], cache_fox_v[l], cache_fox_logf[l], state_ffn_conv[l], prm)
        sample_states.append(st_s)
    gla_p, s5_re_p, s5_im_p, fox_k_p, fox_v_p, fox_logf_p, ffn_conv_p, mem_k_p, mem_v_p = [jnp.stack(z) for z in zip(*prompt_states)]
    gla_s, s5_re_s, s5_im_s, fox_k_s, fox_v_s, fox_logf_s, ffn_conv_s = [jnp.stack(z) for z in zip(*sample_states)]
    return (hp, hs, gla_p, s5_re_p, s5_im_p, fox_k_p, fox_v_p, fox_logf_p, ffn_conv_p, mem_k_p, mem_v_p,
            gla_s, s5_re_s, s5_im_s, fox_k_s, fox_v_s, fox_logf_s, ffn_conv_s)
```

```python
import functools
import math

import jax
import jax.numpy as jnp
from jax import lax
from jax.experimental import pallas as pl
from jax.experimental.pallas import tpu as pltpu

F32 = jnp.float32
BF16 = jnp.bfloat16

D_MODEL = 1024
DEPTH = 2
CHUNK = 64
GLA_HEADS = 4
GLA_DK = 32
GLA_DV = 64
GLA_RANK = 16
GLA_TAU = 16.0
S5_GROUPS = 16
S5_CH = 16
S5_STATE = 64
FOX_HEADS = 8
FOX_HD = 64
MEM_TOKENS = 256
MEM_HEADS = 4
MEM_HD = 256
D_FF = 2816
CONV_W = 3
LN_EPS = 1e-5
DN_ALPHA = (2 * DEPTH) ** 0.25

GLA_QK = GLA_HEADS * GLA_DK
GLA_W = GLA_HEADS * GLA_DV
S5_W = S5_GROUPS * S5_CH
S5_N = S5_GROUPS * S5_STATE
FOX_W = FOX_HEADS * FOX_HD
MEM_W = MEM_HEADS * MEM_HD

LANES = 128
SUBLANES = 8
VMEM_LIMIT = 56 * 1024 * 1024

C_GQK = 0
C_GV = 256
C_GO = 512
C_SU = 768
C_FQ = 1024
C_FK = 1536
C_FV = 2048
C_SM = 2560
N_INR = 2688
SM_FF = 0
SM_GA = FOX_HEADS

FFN_CHUNK = 256
FFN_NCH = D_FF // FFN_CHUNK
S5_TILE = 256
FOX_SCALE = FOX_HD ** -0.5


def _cparams(sem):
    return pltpu.CompilerParams(dimension_semantics=sem, vmem_limit_bytes=VMEM_LIMIT)


def _dot(a, b):
    return jnp.dot(a.astype(BF16), b.astype(BF16), preferred_element_type=F32)


def _dot_nt(a, b):
    return lax.dot_general(a.astype(BF16), b.astype(BF16), (((1,), (1,)), ((), ())),
                           preferred_element_type=F32)


def _dot_tn(a, b):
    return lax.dot_general(a.astype(BF16), b.astype(BF16), (((0,), (0,)), ((), ())),
                           preferred_element_type=F32)


def _split3(x):
    hi = x.astype(BF16)
    r1 = x - hi.astype(F32)
    mid = r1.astype(BF16)
    lo = (r1 - mid.astype(F32)).astype(BF16)
    return hi, mid, lo


def _tri(n):
    r = lax.broadcasted_iota(jnp.int32, (n, n), 0)
    c = lax.broadcasted_iota(jnp.int32, (n, n), 1)
    return (c <= r).astype(BF16)


def _cumsum_rows(x, tri):
    hi, mid, lo = _split3(x)
    d = lambda p: jnp.dot(tri, p, preferred_element_type=F32)
    return d(hi) + (d(mid) + d(lo))


def _layer_norm(x, g, b):
    mu = jnp.mean(x, axis=-1, keepdims=True)
    xc = x - mu
    var = jnp.mean(xc * xc, axis=-1, keepdims=True)
    return xc * lax.rsqrt(var + LN_EPS) * g + b


def _log_sigmoid(x):
    return jnp.minimum(x, 0.0) - jnp.log1p(jnp.exp(-jnp.abs(x)))


def _sigmoid(x):
    return 1.0 / (1.0 + jnp.exp(-x))


def _gelu_tanh(x):
    c = math.sqrt(2.0 / math.pi)
    return 0.5 * x * (1.0 + jnp.tanh(c * (x + 0.044715 * (x * x * x))))


def _row_tile(n, want):
    t = min(n, want)
    assert n % t == 0
    return t


def _ln_kernel(x_ref, g_ref, b_ref, o_ref):
    o_ref[...] = _layer_norm(x_ref[...], g_ref[...], b_ref[...])


def _ln_call(x2, g, b):
    n = x2.shape[0]
    tm = _row_tile(n, 512)
    row = pl.BlockSpec((tm, D_MODEL), lambda i: (i, 0))
    vec = pl.BlockSpec((1, D_MODEL), lambda i: (0, 0))
    return pl.pallas_call(
        _ln_kernel, grid=(n // tm,), in_specs=[row, vec, vec], out_specs=row,
        out_shape=jax.ShapeDtypeStruct((n, D_MODEL), F32),
        compiler_params=_cparams(("parallel",)), name="ln_in")(x2, g, b)


def _in_proj_kernel(x_ref, w_ref, bf_ref, gqk_ref, gv_ref, go_ref, su_ref, fq_ref, fk_ref, fv_ref,
                    fk16_ref, fv16_ref, sm_ref, logf_ref):
    xb = x_ref[...].astype(BF16)
    proj = lambda lo, hi: jnp.dot(xb, w_ref[:, lo:hi], preferred_element_type=F32)
    gqk_ref[...] = proj(C_GQK, C_GV)
    gv_ref[...] = proj(C_GV, C_GO)
    go_ref[...] = proj(C_GO, C_SU)
    su_ref[...] = proj(C_SU, C_FQ)
    fq_ref[...] = (proj(C_FQ, C_FK) * FOX_SCALE).astype(BF16)
    fk = proj(C_FK, C_FV)
    fk_ref[...] = fk
    fk16_ref[...] = fk.astype(BF16)
    fv = proj(C_FV, C_SM)
    fv_ref[...] = fv
    fv16_ref[...] = fv.astype(BF16)
    sm = proj(C_SM, N_INR)
    sm_ref[...] = sm
    logf_ref[...] = _log_sigmoid(sm[:, SM_FF:SM_FF + FOX_HEADS] + bf_ref[...])


def _in_proj_call(x2, w_r, b_f):
    n = x2.shape[0]
    tm = _row_tile(n, 512)
    row = lambda w: pl.BlockSpec((tm, w), lambda i: (i, 0))
    full = lambda a: pl.BlockSpec(a.shape, lambda i: (0,) * a.ndim)
    sds = lambda w, dt: jax.ShapeDtypeStruct((n, w), dt)
    return pl.pallas_call(
        _in_proj_kernel, grid=(n // tm,),
        in_specs=[row(D_MODEL), full(w_r), full(b_f)],
        out_specs=[row(256), row(256), row(256), row(256), row(FOX_W), row(FOX_W), row(FOX_W),
                   row(FOX_W), row(FOX_W), row(LANES), row(FOX_HEADS)],
        out_shape=[sds(256, F32), sds(256, F32), sds(256, F32), sds(256, F32), sds(FOX_W, BF16),
                   sds(FOX_W, F32), sds(FOX_W, F32), sds(FOX_W, BF16), sds(FOX_W, BF16),
                   sds(LANES, F32), sds(FOX_HEADS, F32)],
        compiler_params=_cparams(("parallel",)), name="in_proj")(x2, w_r, b_f)


def _cumsum_kernel(x_ref, o_ref, *, t):
    tri = _tri(CHUNK)

    def body(c, carry):
        r = pl.multiple_of(c * CHUNK, CHUNK)
        cs = _cumsum_rows(x_ref[pl.ds(r, CHUNK), :], tri) + carry
        o_ref[pl.ds(r, CHUNK), :] = cs
        return cs[CHUNK - 1:CHUNK, :]

    lax.fori_loop(0, t // CHUNK, body, jnp.zeros((1, FOX_HEADS), F32))


def _cumsum_call(x3):
    b, t, h = x3.shape
    assert t % CHUNK == 0
    spec = pl.BlockSpec((None, t, h), lambda i: (i, 0, 0))
    return pl.pallas_call(
        functools.partial(_cumsum_kernel, t=t), grid=(b,), in_specs=[spec], out_specs=spec,
        out_shape=jax.ShapeDtypeStruct((b, t, h), F32),
        compiler_params=_cparams(("parallel",)), name="fox_cumsum")(x3)


def _flash_kernel(q_ref, k_ref, v_ref, fq_ref, fk_ref, o_ref, m_sc, l_sc, acc_sc, *, tq, tk, past, nkv):
    i = pl.program_id(1)
    j = pl.program_id(2)
    q_lo = past + i * tq

    @pl.when(j == 0)
    def _():
        m_sc[...] = jnp.full(m_sc.shape, -jnp.inf, F32)
        l_sc[...] = jnp.zeros(l_sc.shape, F32)
        acc_sc[...] = jnp.zeros(acc_sc.shape, F32)

    def compute(masked):
        if masked:
            qpos = q_lo + lax.broadcasted_iota(jnp.int32, (tq, tk), 0)
            kpos = j * tk + lax.broadcasted_iota(jnp.int32, (tq, tk), 1)
            visible = kpos <= qpos
        for h in range(FOX_HEADS):
            sl = slice(h * FOX_HD, (h + 1) * FOX_HD)
            s = _dot_nt(q_ref[:, sl], k_ref[:, sl])
            s = s + fq_ref[:, h:h + 1] - fk_ref[h:h + 1, :]
            if masked:
                s = jnp.where(visible, s, -jnp.inf)
            m_prev = m_sc[h]
            m_new = jnp.maximum(m_prev, jnp.max(s, axis=1, keepdims=True))
            alpha = jnp.exp(m_prev - m_new)
            p = jnp.exp(s - m_new[:, 0:1])
            l_sc[h] = alpha * l_sc[h] + jnp.sum(p, axis=1, keepdims=True)
            acc_sc[:, sl] = acc_sc[:, sl] * alpha[:, 0:FOX_HD] + _dot(p, v_ref[:, sl])
            m_sc[h] = m_new

    k_hi = j * tk + (tk - 1)
    needed = j * tk <= q_lo + (tq - 1)
    fully_visible = k_hi <= q_lo

    @pl.when(jnp.logical_and(needed, fully_visible))
    def _():
        compute(False)

    @pl.when(jnp.logical_and(needed, jnp.logical_not(fully_visible)))
    def _():
        compute(True)

    @pl.when(j == nkv - 1)
    def _():
        for h in range(FOX_HEADS):
            sl = slice(h * FOX_HD, (h + 1) * FOX_HD)
            o_ref[:, sl] = (acc_sc[:, sl] / l_sc[h][:, 0:FOX_HD]).astype(o_ref.dtype)


def _flash_call(q, k, v, cf_q, cf_k_t, past):
    b, tq_all, _ = q.shape
    tk_all = k.shape[1]
    tq = _row_tile(tq_all, 512)
    tk = tk_all if tk_all % 512 else 512
    nq, nkv = tq_all // tq, tk_all // tk

    def kv_idx(bi, i, j):
        last = (past + i * tq + tq - 1) // tk
        return (bi, jnp.minimum(j, last), 0)

    def fk_idx(bi, i, j):
        last = (past + i * tq + tq - 1) // tk
        return (bi, 0, jnp.minimum(j, last))

    qspec = pl.BlockSpec((None, tq, FOX_W), lambda bi, i, j: (bi, i, 0))
    return pl.pallas_call(
        functools.partial(_flash_kernel, tq=tq, tk=tk, past=past, nkv=nkv),
        grid=(b, nq, nkv),
        in_specs=[qspec,
                  pl.BlockSpec((None, tk, FOX_W), kv_idx),
                  pl.BlockSpec((None, tk, FOX_W), kv_idx),
                  pl.BlockSpec((None, tq, FOX_HEADS), lambda bi, i, j: (bi, i, 0)),
                  pl.BlockSpec((None, FOX_HEADS, tk), fk_idx)],
        out_specs=qspec,
        out_shape=jax.ShapeDtypeStruct((b, tq_all, FOX_W), BF16),
        scratch_shapes=[pltpu.VMEM((FOX_HEADS, tq, LANES), F32),
                        pltpu.VMEM((FOX_HEADS, tq, LANES), F32),
                        pltpu.VMEM((tq, FOX_W), F32)],
        compiler_params=_cparams(("parallel", "parallel", "arbitrary")), name="fox_flash")(
            q, k, v, cf_q, cf_k_t)


def _gla_kernel(gqk_ref, gv_ref, go_ref, sm_ref, wa_ref, ba_ref, ng_ref, s0_ref, o_ref, sT_ref, st_sc, *, tb):
    i = pl.program_id(1)

    @pl.when(i == 0)
    def _():
        st_sc[...] = s0_ref[...]

    tri = _tri(CHUNK)
    r_i = lax.broadcasted_iota(jnp.int32, (CHUNK, CHUNK), 0)
    c_i = lax.broadcasted_iota(jnp.int32, (CHUNK, CHUNK), 1)
    causal = c_i <= r_i
    lane_k = lax.broadcasted_iota(jnp.int32, (1, GLA_QK), 1) // GLA_DK
    lane_v = lax.broadcasted_iota(jnp.int32, (1, GLA_W), 1) // GLA_DV
    row_v = lax.broadcasted_iota(jnp.int32, (GLA_W, GLA_QK), 0) // GLA_DV
    col_k = lax.broadcasted_iota(jnp.int32, (GLA_W, GLA_QK), 1) // GLA_DK
    blockdiag = (row_v == col_k).astype(F32)
    rr = lax.broadcasted_iota(jnp.int32, (GLA_W, GLA_W), 0) // GLA_DV
    cc = lax.broadcasted_iota(jnp.int32, (GLA_W, GLA_W), 1) // GLA_DV
    avg = jnp.where(rr == cc, 1.0 / GLA_DV, 0.0).astype(BF16)

    for c in range(tb // CHUNK):
        rows = slice(c * CHUNK, (c + 1) * CHUNK)
        q = gqk_ref[rows, 0:GLA_QK] * (GLA_DK ** -0.5)
        k = gqk_ref[rows, GLA_QK:2 * GLA_QK]
        v = gv_ref[rows, :]
        z = _dot(sm_ref[rows, :], wa_ref[...]) + ba_ref[...]
        log_a = _log_sigmoid(z) / GLA_TAU
        cum = _cumsum_rows(log_a, tri)
        cum_end = cum[CHUNK - 1:CHUNK, :]
        q_dec = q * jnp.exp(cum)
        k_inv = k * jnp.exp(-cum)
        k_end = k * jnp.exp(cum_end - cum)
        st = st_sc[...]
        o = _dot_nt(q_dec, st)
        for h in range(GLA_HEADS):
            qh = jnp.where(lane_k == h, q_dec, 0.0)
            sc = jnp.where(causal, _dot_nt(qh, k_inv), 0.0)
            o = o + jnp.where(lane_v == h, _dot(sc, v), 0.0)
        st_sc[...] = st * jnp.exp(cum_end) + _dot_tn(v, k_end) * blockdiag
        o2 = o * o
        hi = o2.astype(BF16)
        lo = (o2 - hi.astype(F32)).astype(BF16)
        ms = jnp.dot(hi, avg, preferred_element_type=F32) + jnp.dot(lo, avg, preferred_element_type=F32)
        o = o * lax.rsqrt(ms + LN_EPS) * ng_ref[...]
        g = go_ref[rows, :]
        o_ref[rows, :] = (o * (g * _sigmoid(g))).astype(o_ref.dtype)

    sT_ref[...] = st_sc[...]


def _gla_call(gqk, gv, go, sm, wa_p, ba, ng4, s0_t):
    b, t, _ = gqk.shape
    tb = _row_tile(t, 256)
    row = lambda w: pl.BlockSpec((None, tb, w), lambda bi, i: (bi, i, 0))
    full = lambda a: pl.BlockSpec(a.shape, lambda bi, i: (0,) * a.ndim)
    st_spec = pl.BlockSpec((None, GLA_W, GLA_QK), lambda bi, i: (bi, 0, 0))
    return pl.pallas_call(
        functools.partial(_gla_kernel, tb=tb), grid=(b, t // tb),
        in_specs=[row(2 * GLA_QK), row(GLA_W), row(GLA_W), row(LANES), full(wa_p), full(ba), full(ng4), st_spec],
        out_specs=[row(GLA_W), st_spec],
        out_shape=[jax.ShapeDtypeStruct((b, t, GLA_W), BF16), jax.ShapeDtypeStruct((b, GLA_W, GLA_QK), F32)],
        scratch_shapes=[pltpu.VMEM((GLA_W, GLA_QK), F32)],
        compiler_params=_cparams(("parallel", "arbitrary")), name="gla")(gqk, gv, go, sm, wa_p, ba, ng4, s0_t)


def _cmul(ar, ai, br, bi):
    return ar * br - ai * bi, ar * bi + ai * br


def _shift_rows(x, d):
    rolled = pltpu.roll(x, d, 0)
    r = lax.broadcasted_iota(jnp.int32, x.shape, 0)
    return jnp.where(r >= d, rolled, 0.0)


def _s5_prep_kernel(lre_ref, lim_ref, ldt_ref, bre_ref, bim_ref, pre_ref, pim_ref, bbre_ref, bbim_ref):
    lam_re, lam_im = lre_ref[...], lim_ref[...]
    dt = jnp.exp(ldt_ref[...])
    mag = jnp.exp(lam_re * dt)
    ab_re = mag * jnp.cos(lam_im * dt)
    ab_im = mag * jnp.sin(lam_im * dt)
    den = lam_re * lam_re + lam_im * lam_im
    f_re = ((ab_re - 1.0) * lam_re + ab_im * lam_im) / den
    f_im = (ab_im * lam_re - (ab_re - 1.0) * lam_im) / den
    b_re, b_im = bre_ref[...], bim_ref[...]
    bbre_ref[...] = f_re * b_re - f_im * b_im
    bbim_ref[...] = f_re * b_im + f_im * b_re
    pr = jnp.broadcast_to(ab_re, (S5_TILE, S5_N))
    pi = jnp.broadcast_to(ab_im, (S5_TILE, S5_N))
    r = lax.broadcasted_iota(jnp.int32, (S5_TILE, S5_N), 0)
    d = 1
    while d < S5_TILE:
        sr = jnp.where(r >= d, pltpu.roll(pr, d, 0), 1.0)
        si = jnp.where(r >= d, pltpu.roll(pi, d, 0), 0.0)
        pr, pi = _cmul(pr, pi, sr, si)
        d *= 2
    pre_ref[...] = pr
    pim_ref[...] = pi


def _s5_prep_call(lam_re, lam_im, log_dt, b_re_t, b_im_t):
    sds = jax.ShapeDtypeStruct
    return pl.pallas_call(
        _s5_prep_kernel,
        out_shape=[sds((S5_TILE, S5_N), F32), sds((S5_TILE, S5_N), F32),
                   sds((S5_CH, S5_N), F32), sds((S5_CH, S5_N), F32)],
        compiler_params=pltpu.CompilerParams(vmem_limit_bytes=VMEM_LIMIT), name="s5_prep")(
            lam_re, lam_im, log_dt, b_re_t, b_im_t)


def _s5_kernel(u_ref, wbu_ref, pre_ref, pim_ref, wcr_ref, wci_ref, d_ref, wg_ref, bg_ref, h0r_ref, h0i_ref,
               o_ref, hr_ref, hi_ref, cr_sc, ci_sc, *, tt):
    i = pl.program_id(1)

    @pl.when(i == 0)
    def _():
        cr_sc[...] = h0r_ref[...]
        ci_sc[...] = h0i_ref[...]

    u = u_ref[...]
    bu = _dot(u, wbu_ref[...])
    xr, xi = bu[:, 0:S5_N], bu[:, S5_N:2 * S5_N]
    d = 1
    while d < tt:
        ar, ai = pre_ref[d - 1:d, :], pim_ref[d - 1:d, :]
        sr, si = _shift_rows(xr, d), _shift_rows(xi, d)
        mr, mi = _cmul(ar, ai, sr, si)
        xr, xi = xr + mr, xi + mi
        d *= 2
    cr, ci = cr_sc[...], ci_sc[...]
    mr, mi = _cmul(pre_ref[...], pim_ref[...], cr, ci)
    xr, xi = xr + mr, xi + mi
    cr_sc[...] = xr[tt - 1:tt, :]
    ci_sc[...] = xi[tt - 1:tt, :]
    hr_ref[...] = xr[tt - 1:tt, :]
    hi_ref[...] = xi[tt - 1:tt, :]
    y = _dot(xr, wcr_ref[...]) - _dot(xi, wci_ref[...]) + d_ref[...] * u
    z = _gelu_tanh(y)
    o_ref[...] = (z * _sigmoid(_dot(z, wg_ref[...]) + bg_ref[...])).astype(o_ref.dtype)


def _s5_call(u, wbu, pre, pim, wcr, wci, d_skip, wg, bg, h0r, h0i):
    b, t, _ = u.shape
    tt = _row_tile(t, S5_TILE)
    full = lambda a: pl.BlockSpec(a.shape, lambda bi, i: (0,) * a.ndim)
    pspec = pl.BlockSpec((tt, S5_N), lambda bi, i: (0, 0))
    hspec = pl.BlockSpec((None, 1, S5_N), lambda bi, i: (bi, 0, 0))
    row = pl.BlockSpec((None, tt, S5_W), lambda bi, i: (bi, i, 0))
    return pl.pallas_call(
        functools.partial(_s5_kernel, tt=tt), grid=(b, t // tt),
        in_specs=[row, full(wbu), pspec, pspec, full(wcr), full(wci), full(d_skip), full(wg), full(bg), hspec, hspec],
        out_specs=[row, hspec, hspec],
        out_shape=[jax.ShapeDtypeStruct((b, t, S5_W), BF16), jax.ShapeDtypeStruct((b, 1, S5_N), F32),
                   jax.ShapeDtypeStruct((b, 1, S5_N), F32)],
        scratch_shapes=[pltpu.VMEM((1, S5_N), F32), pltpu.VMEM((1, S5_N), F32)],
        compiler_params=_cparams(("parallel", "arbitrary")), name="s5")(
            u, wbu, pre, pim, wcr, wci, d_skip, wg, bg, h0r, h0i)


def _mix_ln_kernel(gla_ref, s5_ref, fox_ref, x_ref, w_ref, g_ref, b_ref, o_ref):
    mixed = jnp.dot(gla_ref[...], w_ref[0:GLA_W, :], preferred_element_type=F32)
    mixed = mixed + jnp.dot(s5_ref[...], w_ref[GLA_W:GLA_W + S5_W, :], preferred_element_type=F32)
    mixed = mixed + jnp.dot(fox_ref[...], w_ref[GLA_W + S5_W:, :], preferred_element_type=F32)
    o_ref[...] = _layer_norm(DN_ALPHA * x_ref[...] + mixed, g_ref[...], b_ref[...])


def _mix_ln_call(o_gla, o_s5, o_fox, x2, w, g, b):
    n = x2.shape[0]
    tm = _row_tile(n, 512)
    row = lambda w_: pl.BlockSpec((tm, w_), lambda i: (i, 0))
    full = lambda a: pl.BlockSpec(a.shape, lambda i: (0,) * a.ndim)
    return pl.pallas_call(
        _mix_ln_kernel, grid=(n // tm,),
        in_specs=[row(GLA_W), row(S5_W), row(FOX_W), row(D_MODEL), full(w), full(g), full(b)],
        out_specs=row(D_MODEL), out_shape=jax.ShapeDtypeStruct((n, D_MODEL), F32),
        compiler_params=_cparams(("parallel",)), name="mix_ln")(o_gla, o_s5, o_fox, x2, w, g, b)


def _mem_kv_kernel(m_ref, wk_ref, wv_ref, k_ref, v_ref):
    mb = m_ref[...].astype(BF16)
    k_ref[...] = jnp.dot(mb, wk_ref[...], preferred_element_type=F32)
    v_ref[...] = jnp.dot(mb, wv_ref[...], preferred_element_type=F32)


def _mem_kv_call(mem2, wk, wv):
    n = mem2.shape[0]
    tm = _row_tile(n, 256)
    row = pl.BlockSpec((tm, D_MODEL), lambda i: (i, 0))
    orow = pl.BlockSpec((tm, MEM_W), lambda i: (i, 0))
    full = lambda a: pl.BlockSpec(a.shape, lambda i: (0,) * a.ndim)
    sds = jax.ShapeDtypeStruct((n, MEM_W), F32)
    return pl.pallas_call(
        _mem_kv_kernel, grid=(n // tm,), in_specs=[row, full(wk), full(wv)], out_specs=[orow, orow],
        out_shape=[sds, sds], compiler_params=_cparams(("parallel",)), name="mem_kv")(mem2, wk, wv)


def _mem_attn_kernel(x_ref, wq_ref, mk_ref, mv_ref, wo_ref, g_ref, b_ref, o_ref):
    x = x_ref[...]
    q = jnp.dot(x.astype(BF16), wq_ref[...], preferred_element_type=F32).astype(BF16)
    mk = mk_ref[...].astype(BF16)
    mv = mv_ref[...].astype(BF16)
    heads = []
    for h in range(MEM_HEADS):
        sl = slice(h * MEM_HD, (h + 1) * MEM_HD)
        s = _dot_nt(q[:, sl], mk[:, sl]) * (MEM_HD ** -0.5)
        m = jnp.max(s, axis=1, keepdims=True)
        e = jnp.exp(s - m)
        p = e / jnp.sum(e, axis=1, keepdims=True)
        heads.append(_dot(p, mv[:, sl]).astype(BF16))
    o = jnp.concatenate(heads, axis=1)
    att = jnp.dot(o, wo_ref[...], preferred_element_type=F32)
    o_ref[...] = _layer_norm(DN_ALPHA * x + att, g_ref[...], b_ref[...])


def _mem_attn_call(x3, wq, mk, mv, wo, g, b):
    bsz, t, _ = x3.shape
    tm = _row_tile(t, 512)
    row = pl.BlockSpec((None, tm, D_MODEL), lambda bi, i: (bi, i, 0))
    mem = pl.BlockSpec((None, MEM_TOKENS, MEM_W), lambda bi, i: (bi, 0, 0))
    full = lambda a: pl.BlockSpec(a.shape, lambda bi, i: (0,) * a.ndim)
    return pl.pallas_call(
        _mem_attn_kernel, grid=(bsz, t // tm),
        in_specs=[row, full(wq), mem, mem, full(wo), full(g), full(b)], out_specs=row,
        out_shape=jax.ShapeDtypeStruct((bsz, t, D_MODEL), F32),
        compiler_params=_cparams(("parallel", "parallel")), name="mem_attn")(x3, wq, mk, mv, wo, g, b)


def _shift_with_halo(u, halo, d):
    rolled = pltpu.roll(u, d, 0)
    r8 = lax.broadcasted_iota(jnp.int32, halo.shape, 0)
    head = jnp.where(r8 < d, pltpu.roll(halo, d, 0), rolled[0:SUBLANES, :])
    return jnp.concatenate([head, rolled[SUBLANES:, :]], axis=0)


def _ffn_kernel(x_ref, xh_ref, wua_ref, wug_ref, pa_ref, pg_ref, cwa_ref, cwg_ref, cba_ref, cbg_ref, wd_ref,
                g_ref, b_ref, o_ref, ta_ref, tg_ref, acc_sc, *, tm):
    i = pl.program_id(1)
    j = pl.program_id(2)

    @pl.when(j == 0)
    def _():
        acc_sc[...] = jnp.zeros(acc_sc.shape, F32)

    xb = x_ref[...].astype(BF16)
    xhb = xh_ref[...].astype(BF16)

    def conv_branch(wu_ref, prev_ref, cw_ref, cb_ref, tail_ref):
        u = jnp.dot(xb, wu_ref[...], preferred_element_type=F32)
        uh = jnp.dot(xhb, wu_ref[...], preferred_element_type=F32)
        halo = jnp.where(i == 0, prev_ref[...], uh)
        tail_ref[...] = u[tm - SUBLANES:, :]
        y = cb_ref[...] + cw_ref[0:1, :] * _shift_with_halo(u, halo, 2)
        y = y + cw_ref[1:2, :] * _shift_with_halo(u, halo, 1)
        return y + cw_ref[2:3, :] * u

    a = conv_branch(wua_ref, pa_ref, cwa_ref, cba_ref, ta_ref)
    g = conv_branch(wug_ref, pg_ref, cwg_ref, cbg_ref, tg_ref)
    hidden = _gelu_tanh(a) * g
    acc_sc[...] += jnp.dot(hidden.astype(BF16), wd_ref[...], preferred_element_type=F32)

    @pl.when(j == FFN_NCH - 1)
    def _():
        o_ref[...] = _layer_norm(DN_ALPHA * x_ref[...] + acc_sc[...], g_ref[...], b_ref[...])


def _ffn_call(x3, w_up, conv_w, conv_b, w_down, prev8, g, b):
    bsz, t, _ = x3.shape
    tm = _row_tile(t, 512)
    c = FFN_CHUNK
    row = pl.BlockSpec((None, tm, D_MODEL), lambda bi, i, j: (bi, i, 0))
    halo = pl.BlockSpec((None, SUBLANES, D_MODEL),
                        lambda bi, i, j: (bi, jnp.maximum(i * (tm // SUBLANES) - 1, 0), 0))
    col_a = lambda r: pl.BlockSpec((r, c), lambda bi, i, j: (0, j))
    col_g = lambda r: pl.BlockSpec((r, c), lambda bi, i, j: (0, FFN_NCH + j))
    prev_a = pl.BlockSpec((None, SUBLANES, c), lambda bi, i, j: (bi, 0, j))
    prev_g = pl.BlockSpec((None, SUBLANES, c), lambda bi, i, j: (bi, 0, FFN_NCH + j))
    vec = pl.BlockSpec((1, D_MODEL), lambda bi, i, j: (0, 0))
    tail_spec = pl.BlockSpec((None, None, SUBLANES, c), lambda bi, i, j: (bi, i, 0, j))
    tail = jax.ShapeDtypeStruct((bsz, t // tm, SUBLANES, D_FF), F32)
    return pl.pallas_call(
        functools.partial(_ffn_kernel, tm=tm), grid=(bsz, t // tm, FFN_NCH),
        in_specs=[row, halo, col_a(D_MODEL), col_g(D_MODEL), prev_a, prev_g, col_a(CONV_W), col_g(CONV_W),
                  col_a(1), col_g(1), pl.BlockSpec((c, D_MODEL), lambda bi, i, j: (j, 0)), vec, vec],
        out_specs=[row, tail_spec, tail_spec],
        out_shape=[jax.ShapeDtypeStruct((bsz, t, D_MODEL), F32), tail, tail],
        scratch_shapes=[pltpu.VMEM((tm, D_MODEL), F32)],
        compiler_params=_cparams(("parallel", "arbitrary", "arbitrary")), name="conv_ffn")(
            x3, x3, w_up, w_up, prev8, prev8, conv_w, conv_w, conv_b, conv_b, w_down, g, b)


def _block_diag(blocks):
    g, r, c = blocks.shape
    eye = jnp.eye(g, dtype=blocks.dtype)
    return (blocks[:, :, None, :] * eye[:, None, :, None]).reshape(g * r, g * c)


def _prep_layer(l, p):
    w = p['w_in'][l]
    offs = [0, 128, 256, 512, 768, 784, 1040, 1552, 2064, 2576, 2584]
    gq, gk, gv, go, ga, su, fq, fk, fv, ff = [w[:, offs[n]:offs[n + 1]] for n in range(10)]
    pad = jnp.zeros((D_MODEL, LANES - FOX_HEADS - GLA_RANK), F32)
    w_r = jnp.concatenate([gq, gk, gv, go, su, fq, fk, fv, ff, ga, pad], axis=1).astype(BF16)
    wa_p = jnp.zeros((LANES, GLA_QK), F32).at[SM_GA:SM_GA + GLA_RANK].set(p['gla_w_a2'][l]).astype(BF16)
    row = lambda a: a.reshape(1, -1)
    flat = lambda a: a.reshape(1, S5_N)
    b_t = lambda a: jnp.transpose(a, (2, 0, 1)).reshape(S5_CH, S5_N)
    pre, pim, bb_re, bb_im = _s5_prep_call(flat(p['s5_lam_re'][l]), flat(p['s5_lam_im'][l]),
                                           flat(p['s5_log_dt'][l]), b_t(p['s5_b_re'][l]), b_t(p['s5_b_im'][l]))
    bd_in = lambda bb: _block_diag(jnp.transpose(bb.reshape(S5_CH, S5_GROUPS, S5_STATE), (1, 0, 2)))
    wbu = jnp.concatenate([bd_in(bb_re), bd_in(bb_im)], axis=1).astype(BF16)
    bd_out = lambda cm: _block_diag(jnp.transpose(cm, (0, 2, 1))).astype(BF16)
    return dict(
        w_r=w_r, b_f=row(p['fox_b_f'][l]), wa_p=wa_p, ba=row(p['gla_b_a'][l]),
        ng4=jnp.tile(p['gla_norm_g'][l], GLA_HEADS).reshape(1, GLA_W),
        wbu=wbu, pre=pre, pim=pim, wcr=bd_out(p['s5_c_re'][l]), wci=bd_out(p['s5_c_im'][l]),
        d_skip=row(p['s5_d'][l]), wg=p['s5_w_glu'][l].astype(BF16), bg=row(p['s5_b_glu'][l]),
        w_mix=p['w_mix_out'][l].astype(BF16), ln1_g=row(p['ln1_g'][l]), ln1_b=row(p['ln1_b'][l]),
        wq=p['mem_w_q'][l].astype(BF16), wk=p['mem_w_k'][l].astype(BF16), wv=p['mem_w_v'][l].astype(BF16),
        wo=p['mem_w_o'][l].astype(BF16), ln2_g=row(p['ln2_g'][l]), ln2_b=row(p['ln2_b'][l]),
        w_up=p['ffn_w_up'][l].astype(BF16), conv_w=p['ffn_conv_w'][l], conv_b=row(p['ffn_conv_b'][l]),
        w_down=p['ffn_w_down'][l].astype(BF16), ln3_g=row(p['ln3_g'][l]), ln3_b=row(p['ln3_b'][l]))


def _gla_state_in(s0):
    eye = jnp.eye(GLA_HEADS, dtype=s0.dtype)
    st = jnp.transpose(s0, (0, 1, 3, 2))
    full = st[:, :, :, None, :] * eye[None, :, None, :, None]
    return full.reshape(s0.shape[0], GLA_W, GLA_QK)


def _gla_state_out(st):
    s5 = st.reshape(st.shape[0], GLA_HEADS, GLA_DV, GLA_HEADS, GLA_DK)
    diag = jnp.stack([s5[:, h, :, h, :] for h in range(GLA_HEADS)], axis=1)
    return jnp.transpose(diag, (0, 1, 3, 2))


def _layer(x3, mem_k, mem_v, gla_s0, s5_h0_re, s5_h0_im, past_k, past_v, past_logf, conv_prev, lp):
    bsz, t, _ = x3.shape
    n = bsz * t
    x2 = x3.reshape(n, D_MODEL)
    gqk, gv, go, su, fq16, fk, fv, fk16, fv16, sm, logf = _in_proj_call(x2, lp['w_r'], lp['b_f'])
    r3 = lambda a: a.reshape(bsz, t, a.shape[-1])

    o_gla, st_t = _gla_call(r3(gqk), r3(gv), r3(go), r3(sm), lp['wa_p'], lp['ba'], lp['ng4'], _gla_state_in(gla_s0))
    o_s5, h_re, h_im = _s5_call(r3(su), lp['wbu'], lp['pre'], lp['pim'], lp['wcr'], lp['wci'], lp['d_skip'],
                                lp['wg'], lp['bg'], s5_h0_re.reshape(bsz, 1, S5_N), s5_h0_im.reshape(bsz, 1, S5_N))

    logf3 = r3(logf)
    if past_k is None:
        past = 0
        k_all, v_all, logf_all = r3(fk16), r3(fv16), logf3
    else:
        past = past_k.shape[1]
        k_all = jnp.concatenate([past_k.reshape(bsz, past, FOX_W), r3(fk)], axis=1)
        v_all = jnp.concatenate([past_v.reshape(bsz, past, FOX_W), r3(fv)], axis=1)
        logf_all = jnp.concatenate([past_logf, logf3], axis=1)
    cum_f = _cumsum_call(logf_all)
    o_fox = _flash_call(r3(fq16), k_all, v_all, cum_f[:, past:], jnp.swapaxes(cum_f, 1, 2), past)

    x1 = _mix_ln_call(o_gla.reshape(n, GLA_W), o_s5.reshape(n, S5_W), o_fox.reshape(n, FOX_W), x2,
                      lp['w_mix'], lp['ln1_g'], lp['ln1_b'])
    x2b = _mem_attn_call(x1.reshape(bsz, t, D_MODEL), lp['wq'], mem_k, mem_v, lp['wo'], lp['ln2_g'], lp['ln2_b'])
    prev8 = jnp.concatenate([jnp.zeros((bsz, SUBLANES - (CONV_W - 1), 2 * D_FF), F32), conv_prev], axis=1)
    y3, tail_a, tail_g = _ffn_call(x2b, lp['w_up'], lp['conv_w'], lp['conv_b'], lp['w_down'], prev8,
                                   lp['ln3_g'], lp['ln3_b'])
    conv_new = jnp.concatenate([tail_a[:, -1], tail_g[:, -1]], axis=2)[:, SUBLANES - (CONV_W - 1):, :]
    states = (_gla_state_out(st_t), h_re.reshape(bsz, S5_GROUPS, S5_STATE), h_im.reshape(bsz, S5_GROUPS, S5_STATE),
              fk.reshape(bsz, t, FOX_HEADS, FOX_HD), fv.reshape(bsz, t, FOX_HEADS, FOX_HD), logf3, conv_new)
    return y3, states


def kernel(x_prompt, x_sample, mem_prompt, state_gla, state_s5_re, state_s5_im, cache_fox_k, cache_fox_v,
           cache_fox_logf, cache_mem_k, cache_mem_v, state_ffn_conv, ln_in_g, ln_in_b, w_in, gla_w_a2, gla_b_a,
           gla_norm_g, s5_lam_re, s5_lam_im, s5_log_dt, s5_b_re, s5_b_im, s5_c_re, s5_c_im, s5_d, s5_w_glu,
           s5_b_glu, fox_b_f, w_mix_out, ln1_g, ln1_b, mem_w_q, mem_w_k, mem_w_v, mem_w_o, ln2_g, ln2_b,
           ffn_w_up, ffn_conv_w, ffn_conv_b, ffn_w_down, ln3_g, ln3_b):
    params = dict(w_in=w_in, gla_w_a2=gla_w_a2, gla_b_a=gla_b_a, gla_norm_g=gla_norm_g, s5_lam_re=s5_lam_re,
                  s5_lam_im=s5_lam_im, s5_log_dt=s5_log_dt, s5_b_re=s5_b_re, s5_b_im=s5_b_im, s5_c_re=s5_c_re,
                  s5_c_im=s5_c_im, s5_d=s5_d, s5_w_glu=s5_w_glu, s5_b_glu=s5_b_glu, fox_b_f=fox_b_f,
                  w_mix_out=w_mix_out, ln1_g=ln1_g, ln1_b=ln1_b, mem_w_q=mem_w_q, mem_w_k=mem_w_k,
                  mem_w_v=mem_w_v, mem_w_o=mem_w_o, ln2_g=ln2_g, ln2_b=ln2_b, ffn_w_up=ffn_w_up,
                  ffn_conv_w=ffn_conv_w, ffn_conv_b=ffn_conv_b, ffn_w_down=ffn_w_down, ln3_g=ln3_g, ln3_b=ln3_b)
    bp, tp, _ = x_prompt.shape
    bs, ts, _ = x_sample.shape
    g_in, b_in = ln_in_g.reshape(1, D_MODEL), ln_in_b.reshape(1, D_MODEL)
    hp = _ln_call(x_prompt.reshape(bp * tp, D_MODEL), g_in, b_in).reshape(bp, tp, D_MODEL)
    hs = _ln_call(x_sample.reshape(bs * ts, D_MODEL), g_in, b_in).reshape(bs, ts, D_MODEL)
    mem2 = mem_prompt.reshape(bp * MEM_TOKENS, D_MODEL)
    zero_gla = jnp.zeros((bp, GLA_HEADS, GLA_DK, GLA_DV), F32)
    zero_s5 = jnp.zeros((bp, S5_GROUPS, S5_STATE), F32)
    zero_conv = jnp.zeros((bp, CONV_W - 1, 2 * D_FF), F32)
    prompt_states, sample_states = [], []
    for l in range(DEPTH):
        lp = _prep_layer(l, params)
        mk2, mv2 = _mem_kv_call(mem2, lp['wk'], lp['wv'])
        mk_p = mk2.reshape(bp, MEM_TOKENS, MEM_W)
        mv_p = mv2.reshape(bp, MEM_TOKENS, MEM_W)
        hp, st_p = _layer(hp, mk_p, mv_p, zero_gla, zero_s5, zero_s5, None, None, None, zero_conv, lp)
        prompt_states.append(st_p + (mk_p.reshape(bp, MEM_TOKENS, MEM_HEADS, MEM_HD),
                                     mv_p.reshape(bp, MEM_TOKENS, MEM_HEADS, MEM_HD)))
        hs, st_s = _layer(hs, cache_mem_k[l].reshape(bs, MEM_TOKENS, MEM_W),
                          cache_mem_v[l].reshape(bs, MEM_TOKENS, MEM_W), state_gla[l], state_s5_re[l],
                          state_s5_im[l], cache_fox_k[l], cache_fox_v[l], cache_fox_logf[l], state_ffn_conv[l], lp)
        sample_states.append(st_s)
    p_out = [jnp.stack(z) for z in zip(*prompt_states)]
    s_out = [jnp.stack(z) for z in zip(*sample_states)]
    return (hp, hs, *p_out, *s_out)
```

```python
import functools
import math

import jax
import jax.numpy as jnp
from jax import lax
from jax.experimental import pallas as pl
from jax.experimental.pallas import tpu as pltpu

F32 = jnp.float32
BF16 = jnp.bfloat16

D_MODEL = 1024
DEPTH = 2
CHUNK = 64
GLA_HEADS = 4
GLA_DK = 32
GLA_DV = 64
GLA_RANK = 16
GLA_TAU = 16.0
S5_GROUPS = 16
S5_CH = 16
S5_STATE = 64
FOX_HEADS = 8
FOX_HD = 64
MEM_TOKENS = 256
MEM_HEADS = 4
MEM_HD = 256
D_FF = 2816
CONV_W = 3
LN_EPS = 1e-5
DN_ALPHA = (2 * DEPTH) ** 0.25

GLA_QK = GLA_HEADS * GLA_DK
GLA_W = GLA_HEADS * GLA_DV
S5_W = S5_GROUPS * S5_CH
S5_N = S5_GROUPS * S5_STATE
FOX_W = FOX_HEADS * FOX_HD
MEM_W = MEM_HEADS * MEM_HD

LANES = 128
SUBLANES = 8
VMEM_LIMIT = 56 * 1024 * 1024

C_GQK = 0
C_GV = 256
C_GO = 512
C_SU = 768
C_FQ = 1024
C_FK = 1536
C_FV = 2048
C_SM = 2560
N_INR = 2688
SM_FF = 0
SM_GA = FOX_HEADS

FFN_CHUNK = 256
FFN_NCH = D_FF // FFN_CHUNK
S5_TILE = 256
FOX_SCALE = FOX_HD ** -0.5
LOG2E = math.log2(math.e)
FOX_AUG_W = FOX_HEADS * LANES
FOX_NPIECE = 3
FOX_TQ = 512
FOX_TK = 512


def _cparams(sem):
    return pltpu.CompilerParams(dimension_semantics=sem, vmem_limit_bytes=VMEM_LIMIT)


def _dot(a, b):
    return jnp.dot(a.astype(BF16), b.astype(BF16), preferred_element_type=F32)


def _dot_nt(a, b):
    return lax.dot_general(a.astype(BF16), b.astype(BF16), (((1,), (1,)), ((), ())),
                           preferred_element_type=F32)


def _dot_tn(a, b):
    return lax.dot_general(a.astype(BF16), b.astype(BF16), (((0,), (0,)), ((), ())),
                           preferred_element_type=F32)


def _split3(x):
    hi = x.astype(BF16)
    r1 = x - hi.astype(F32)
    mid = r1.astype(BF16)
    lo = (r1 - mid.astype(F32)).astype(BF16)
    return hi, mid, lo


def _tri(n):
    r = lax.broadcasted_iota(jnp.int32, (n, n), 0)
    c = lax.broadcasted_iota(jnp.int32, (n, n), 1)
    return (c <= r).astype(BF16)


def _cumsum_rows(x, tri):
    hi, mid, lo = _split3(x)
    d = lambda p: jnp.dot(tri, p, preferred_element_type=F32)
    return d(hi) + (d(mid) + d(lo))


def _layer_norm(x, g, b):
    mu = jnp.mean(x, axis=-1, keepdims=True)
    xc = x - mu
    var = jnp.mean(xc * xc, axis=-1, keepdims=True)
    return xc * lax.rsqrt(var + LN_EPS) * g + b


def _log_sigmoid(x):
    return jnp.minimum(x, 0.0) - jnp.log1p(jnp.exp(-jnp.abs(x)))


def _sigmoid(x):
    return 1.0 / (1.0 + jnp.exp(-x))


def _gelu_tanh(x):
    c = math.sqrt(2.0 / math.pi)
    return 0.5 * x * (1.0 + jnp.tanh(c * (x + 0.044715 * (x * x * x))))


def _row_tile(n, want):
    t = min(n, want)
    assert n % t == 0
    return t


def _ln_kernel(x_ref, g_ref, b_ref, o_ref):
    o_ref[...] = _layer_norm(x_ref[...], g_ref[...], b_ref[...])


def _ln_call(x2, g, b):
    n = x2.shape[0]
    tm = _row_tile(n, 512)
    row = pl.BlockSpec((tm, D_MODEL), lambda i: (i, 0))
    vec = pl.BlockSpec((1, D_MODEL), lambda i: (0, 0))
    return pl.pallas_call(
        _ln_kernel, grid=(n // tm,), in_specs=[row, vec, vec], out_specs=row,
        out_shape=jax.ShapeDtypeStruct((n, D_MODEL), F32),
        compiler_params=_cparams(("parallel",)), name="ln_in")(x2, g, b)


def _in_proj_kernel(x_ref, w_ref, bf_ref, gqk_ref, gv_ref, go_ref, su_ref, fk_ref, fv_ref, sm_ref, logf_ref,
                    fq_ref):
    xb = x_ref[...].astype(BF16)
    proj = lambda lo, hi: jnp.dot(xb, w_ref[:, lo:hi], preferred_element_type=F32)
    gqk_ref[...] = proj(C_GQK, C_GV)
    gv_ref[...] = proj(C_GV, C_GO)
    go_ref[...] = proj(C_GO, C_SU)
    su_ref[...] = proj(C_SU, C_FQ)
    fk_ref[...] = proj(C_FK, C_FV)
    fv_ref[...] = proj(C_FV, C_SM)
    sm = proj(C_SM, N_INR)
    sm_ref[...] = sm
    logf_ref[...] = _log_sigmoid(sm[:, SM_FF:SM_FF + FOX_HEADS] + bf_ref[...])
    fq_ref[...] = (proj(C_FQ, C_FK) * FOX_SCALE).astype(BF16)


def _in_proj_t_kernel(x_ref, w_ref, bf_ref, wqt_ref, wkp_ref, wvt_ref, gqk_ref, gv_ref, go_ref, su_ref, fk_ref,
                      fv_ref, sm_ref, logf_ref, qt_ref, kp_ref, vt_ref):
    xb = x_ref[...].astype(BF16)
    proj = lambda lo, hi: jnp.dot(xb, w_ref[:, lo:hi], preferred_element_type=F32)
    gqk_ref[...] = proj(C_GQK, C_GV)
    gv_ref[...] = proj(C_GV, C_GO)
    go_ref[...] = proj(C_GO, C_SU)
    su_ref[...] = proj(C_SU, C_FQ)
    fk_ref[...] = proj(C_FK, C_FV)
    fv_ref[...] = proj(C_FV, C_SM)
    sm = proj(C_SM, N_INR)
    sm_ref[...] = sm
    logf_ref[...] = _log_sigmoid(sm[:, SM_FF:SM_FF + FOX_HEADS] + bf_ref[...])
    qt_ref[...] = (_dot_nt(wqt_ref[...], xb) * (FOX_SCALE * LOG2E)).astype(BF16)
    kp_ref[...] = jnp.dot(xb, wkp_ref[...], preferred_element_type=F32).astype(BF16)
    vt_ref[...] = _dot_nt(wvt_ref[...], xb).astype(BF16)


def _in_proj_call(x3, lp, fox_t):
    bsz, t, _ = x3.shape
    tm = _row_tile(t, 512)
    row = lambda w: pl.BlockSpec((None, tm, w), lambda bi, i: (bi, i, 0))
    col = lambda r: pl.BlockSpec((None, r, tm), lambda bi, i: (bi, 0, i))
    full = lambda a: pl.BlockSpec(a.shape, lambda bi, i: (0,) * a.ndim)
    sds = lambda w, dt: jax.ShapeDtypeStruct((bsz, t, w), dt)
    in_arrays = [x3, lp['w_r'], lp['b_f']]
    out_specs = [row(256), row(256), row(256), row(256), row(FOX_W), row(FOX_W), row(LANES), row(FOX_HEADS)]
    out_shape = [sds(256, F32), sds(256, F32), sds(256, F32), sds(256, F32), sds(FOX_W, F32), sds(FOX_W, F32),
                 sds(LANES, F32), sds(FOX_HEADS, F32)]
    if fox_t:
        body = _in_proj_t_kernel
        in_arrays += [lp['wqt_pad'], lp['wk_pad'], lp['wvt']]
        out_specs += [col(FOX_AUG_W), row(FOX_AUG_W), col(FOX_W)]
        out_shape += [jax.ShapeDtypeStruct((bsz, FOX_AUG_W, t), BF16), sds(FOX_AUG_W, BF16),
                      jax.ShapeDtypeStruct((bsz, FOX_W, t), BF16)]
    else:
        body = _in_proj_kernel
        out_specs += [row(FOX_W)]
        out_shape += [sds(FOX_W, BF16)]
    in_specs = [row(D_MODEL)] + [full(a) for a in in_arrays[1:]]
    return pl.pallas_call(
        body, grid=(bsz, t // tm), in_specs=in_specs, out_specs=out_specs, out_shape=out_shape,
        compiler_params=_cparams(("parallel", "parallel")), name="in_proj")(*in_arrays)


def _cumsum_kernel(x_ref, o_ref, *, t):
    tri = _tri(CHUNK)

    def body(c, carry):
        r = pl.multiple_of(c * CHUNK, CHUNK)
        cs = _cumsum_rows(x_ref[pl.ds(r, CHUNK), :], tri) + carry
        o_ref[pl.ds(r, CHUNK), :] = cs
        return cs[CHUNK - 1:CHUNK, :]

    lax.fori_loop(0, t // CHUNK, body, jnp.zeros((1, FOX_HEADS), F32))


def _cumsum_call(x3):
    b, t, h = x3.shape
    assert t % CHUNK == 0
    spec = pl.BlockSpec((None, t, h), lambda i: (i, 0, 0))
    return pl.pallas_call(
        functools.partial(_cumsum_kernel, t=t), grid=(b,), in_specs=[spec], out_specs=spec,
        out_shape=jax.ShapeDtypeStruct((b, t, h), F32),
        compiler_params=_cparams(("parallel",)), name="fox_cumsum")(x3)


def _fox_aug_kernel(logf_ref, kp_ref, qt_ref, ko_ref, qo_ref, carry_sc, *, tt):
    @pl.when(pl.program_id(1) == 0)
    def _():
        carry_sc[...] = jnp.zeros(carry_sc.shape, F32)

    cum = _cumsum_rows(logf_ref[...], _tri(tt)) + carry_sc[...]
    carry_sc[...] = cum[tt - 1:tt, :]
    pieces = [p.astype(F32) for p in _split3(cum * LOG2E)]

    lane = lax.broadcasted_iota(jnp.int32, (FOX_HEADS, FOX_AUG_W), 1)
    base_k = lax.broadcasted_iota(jnp.int32, (FOX_HEADS, FOX_AUG_W), 0) * LANES + FOX_HD + FOX_NPIECE
    aug_k = jnp.zeros((tt, FOX_AUG_W), F32)
    for p in range(FOX_NPIECE):
        aug_k = aug_k - jnp.dot(pieces[p], (lane == base_k + p).astype(F32), preferred_element_type=F32)
    off_k = lax.broadcasted_iota(jnp.int32, (1, FOX_AUG_W), 1) & (LANES - 1)
    ones_k = jnp.logical_and(off_k >= FOX_HD, off_k < FOX_HD + FOX_NPIECE).astype(F32)
    ko_ref[...] = (kp_ref[...].astype(F32) + aug_k + ones_k).astype(BF16)

    row = lax.broadcasted_iota(jnp.int32, (FOX_AUG_W, FOX_HEADS), 0)
    base_q = lax.broadcasted_iota(jnp.int32, (FOX_AUG_W, FOX_HEADS), 1) * LANES + FOX_HD
    aug_q = jnp.zeros((FOX_AUG_W, tt), F32)
    for p in range(FOX_NPIECE):
        aug_q = aug_q + lax.dot_general((row == base_q + p).astype(F32), pieces[p], (((1,), (1,)), ((), ())),
                                        preferred_element_type=F32)
    off_q = lax.broadcasted_iota(jnp.int32, (FOX_AUG_W, 1), 0) & (LANES - 1)
    ones_q = jnp.logical_and(off_q >= FOX_HD + FOX_NPIECE, off_q < FOX_HD + 2 * FOX_NPIECE).astype(F32)
    qo_ref[...] = (qt_ref[...].astype(F32) + aug_q + ones_q).astype(BF16)


def _fox_aug_call(logf3, k_pad, qt_pad):
    bsz, t, _ = logf3.shape
    tt = _row_tile(t, 512)
    kspec = pl.BlockSpec((None, tt, FOX_AUG_W), lambda bi, i: (bi, i, 0))
    qspec = pl.BlockSpec((None, FOX_AUG_W, tt), lambda bi, i: (bi, 0, i))
    return pl.pallas_call(
        functools.partial(_fox_aug_kernel, tt=tt), grid=(bsz, t // tt),
        in_specs=[pl.BlockSpec((None, tt, FOX_HEADS), lambda bi, i: (bi, i, 0)), kspec, qspec],
        out_specs=[kspec, qspec],
        out_shape=[jax.ShapeDtypeStruct(k_pad.shape, BF16), jax.ShapeDtypeStruct(qt_pad.shape, BF16)],
        scratch_shapes=[pltpu.VMEM((1, FOX_HEADS), F32)],
        input_output_aliases={1: 0, 2: 1},
        compiler_params=_cparams(("parallel", "arbitrary")), name="fox_aug")(logf3, k_pad, qt_pad)


def _flash_t_kernel(qt_ref, k_ref, vt_ref, o_ref, m_sc, l_sc, acc_sc, *, tq, tk, nkv):
    i = pl.program_id(1)
    j = pl.program_id(2)

    @pl.when(j == 0)
    def _():
        m_sc[...] = jnp.full(m_sc.shape, -jnp.inf, F32)
        l_sc[...] = jnp.zeros(l_sc.shape, F32)
        acc_sc[...] = jnp.zeros(acc_sc.shape, F32)

    def compute(masked):
        if masked:
            kpos = j * tk + lax.broadcasted_iota(jnp.int32, (tk, tq), 0)
            qpos = i * tq + lax.broadcasted_iota(jnp.int32, (tk, tq), 1)
            visible = kpos <= qpos
        for h in range(FOX_HEADS):
            s_t = jnp.dot(k_ref[:, h * LANES:(h + 1) * LANES], qt_ref[h * LANES:(h + 1) * LANES, :],
                          preferred_element_type=F32)
            if masked:
                s_t = jnp.where(visible, s_t, -jnp.inf)
            m_prev = m_sc[h:h + 1, :]
            m_new = jnp.maximum(m_prev, jnp.max(s_t, axis=0, keepdims=True))
            alpha = jnp.exp2(m_prev - m_new)
            p_t = jnp.exp2(s_t - m_new)
            l_sc[h:h + 1, :] = alpha * l_sc[h:h + 1, :] + jnp.sum(p_t, axis=0, keepdims=True)
            rows = slice(h * FOX_HD, (h + 1) * FOX_HD)
            acc_sc[rows, :] = acc_sc[rows, :] * alpha + jnp.dot(vt_ref[rows, :], p_t.astype(BF16),
                                                                preferred_element_type=F32)
            m_sc[h:h + 1, :] = m_new

    needed = j * tk <= i * tq + (tq - 1)
    fully_visible = j * tk + (tk - 1) <= i * tq

    @pl.when(jnp.logical_and(needed, fully_visible))
    def _():
        compute(False)

    @pl.when(jnp.logical_and(needed, jnp.logical_not(fully_visible)))
    def _():
        compute(True)

    @pl.when(j == nkv - 1)
    def _():
        for h in range(FOX_HEADS):
            rows = slice(h * FOX_HD, (h + 1) * FOX_HD)
            acc_sc[rows, :] = acc_sc[rows, :] / l_sc[h:h + 1, :]
        o_ref[...] = acc_sc[...].T.astype(o_ref.dtype)


def _flash_t_call(qt_aug, k_aug, vt):
    bsz, _, t = qt_aug.shape
    tq, tk = _row_tile(t, FOX_TQ), _row_tile(t, FOX_TK)
    nq, nkv = t // tq, t // tk
    last = lambda i: (i * tq + tq - 1) // tk
    return pl.pallas_call(
        functools.partial(_flash_t_kernel, tq=tq, tk=tk, nkv=nkv), grid=(bsz, nq, nkv),
        in_specs=[pl.BlockSpec((None, FOX_AUG_W, tq), lambda bi, i, j: (bi, 0, i)),
                  pl.BlockSpec((None, tk, FOX_AUG_W), lambda bi, i, j: (bi, jnp.minimum(j, last(i)), 0)),
                  pl.BlockSpec((None, FOX_W, tk), lambda bi, i, j: (bi, 0, jnp.minimum(j, last(i))))],
        out_specs=pl.BlockSpec((None, tq, FOX_W), lambda bi, i, j: (bi, i, 0)),
        out_shape=jax.ShapeDtypeStruct((bsz, t, FOX_W), BF16),
        scratch_shapes=[pltpu.VMEM((FOX_HEADS, tq), F32), pltpu.VMEM((FOX_HEADS, tq), F32),
                        pltpu.VMEM((FOX_W, tq), F32)],
        compiler_params=_cparams(("parallel", "parallel", "arbitrary")), name="fox_flash_t")(qt_aug, k_aug, vt)


def _flash_kernel(q_ref, k_ref, v_ref, fq_ref, fk_ref, o_ref, m_sc, l_sc, acc_sc, *, tq, tk, past, nkv):
    i = pl.program_id(1)
    j = pl.program_id(2)
    q_lo = past + i * tq

    @pl.when(j == 0)
    def _():
        m_sc[...] = jnp.full(m_sc.shape, -jnp.inf, F32)
        l_sc[...] = jnp.zeros(l_sc.shape, F32)
        acc_sc[...] = jnp.zeros(acc_sc.shape, F32)

    def compute(masked):
        if masked:
            qpos = q_lo + lax.broadcasted_iota(jnp.int32, (tq, tk), 0)
            kpos = j * tk + lax.broadcasted_iota(jnp.int32, (tq, tk), 1)
            visible = kpos <= qpos
        for h in range(FOX_HEADS):
            sl = slice(h * FOX_HD, (h + 1) * FOX_HD)
            s = _dot_nt(q_ref[:, sl], k_ref[:, sl])
            s = s + fq_ref[:, h:h + 1] - fk_ref[h:h + 1, :]
            if masked:
                s = jnp.where(visible, s, -jnp.inf)
            m_prev = m_sc[h]
            m_new = jnp.maximum(m_prev, jnp.max(s, axis=1, keepdims=True))
            alpha = jnp.exp(m_prev - m_new)
            p = jnp.exp(s - m_new[:, 0:1])
            l_sc[h] = alpha * l_sc[h] + jnp.sum(p, axis=1, keepdims=True)
            acc_sc[:, sl] = acc_sc[:, sl] * alpha[:, 0:FOX_HD] + _dot(p, v_ref[:, sl])
            m_sc[h] = m_new

    k_hi = j * tk + (tk - 1)
    needed = j * tk <= q_lo + (tq - 1)
    fully_visible = k_hi <= q_lo

    @pl.when(jnp.logical_and(needed, fully_visible))
    def _():
        compute(False)

    @pl.when(jnp.logical_and(needed, jnp.logical_not(fully_visible)))
    def _():
        compute(True)

    @pl.when(j == nkv - 1)
    def _():
        for h in range(FOX_HEADS):
            sl = slice(h * FOX_HD, (h + 1) * FOX_HD)
            o_ref[:, sl] = (acc_sc[:, sl] / l_sc[h][:, 0:FOX_HD]).astype(o_ref.dtype)


def _flash_call(q, k, v, cf_q, cf_k_t, past):
    b, tq_all, _ = q.shape
    tk_all = k.shape[1]
    tq = _row_tile(tq_all, 512)
    tk = tk_all if tk_all % 512 else 512
    nq, nkv = tq_all // tq, tk_all // tk

    def kv_idx(bi, i, j):
        last = (past + i * tq + tq - 1) // tk
        return (bi, jnp.minimum(j, last), 0)

    def fk_idx(bi, i, j):
        last = (past + i * tq + tq - 1) // tk
        return (bi, 0, jnp.minimum(j, last))

    qspec = pl.BlockSpec((None, tq, FOX_W), lambda bi, i, j: (bi, i, 0))
    return pl.pallas_call(
        functools.partial(_flash_kernel, tq=tq, tk=tk, past=past, nkv=nkv),
        grid=(b, nq, nkv),
        in_specs=[qspec,
                  pl.BlockSpec((None, tk, FOX_W), kv_idx),
                  pl.BlockSpec((None, tk, FOX_W), kv_idx),
                  pl.BlockSpec((None, tq, FOX_HEADS), lambda bi, i, j: (bi, i, 0)),
                  pl.BlockSpec((None, FOX_HEADS, tk), fk_idx)],
        out_specs=qspec,
        out_shape=jax.ShapeDtypeStruct((b, tq_all, FOX_W), BF16),
        scratch_shapes=[pltpu.VMEM((FOX_HEADS, tq, LANES), F32),
                        pltpu.VMEM((FOX_HEADS, tq, LANES), F32),
                        pltpu.VMEM((tq, FOX_W), F32)],
        compiler_params=_cparams(("parallel", "parallel", "arbitrary")), name="fox_flash")(
            q, k, v, cf_q, cf_k_t)


def _gla_kernel(gqk_ref, gv_ref, go_ref, sm_ref, wa_ref, ba_ref, ng_ref, s0_ref, o_ref, sT_ref, st_sc, *, tb):
    i = pl.program_id(1)

    @pl.when(i == 0)
    def _():
        st_sc[...] = s0_ref[...]

    tri = _tri(CHUNK)
    r_i = lax.broadcasted_iota(jnp.int32, (CHUNK, CHUNK), 0)
    c_i = lax.broadcasted_iota(jnp.int32, (CHUNK, CHUNK), 1)
    causal = c_i <= r_i
    lane_k = lax.broadcasted_iota(jnp.int32, (1, GLA_QK), 1) // GLA_DK
    lane_v = lax.broadcasted_iota(jnp.int32, (1, GLA_W), 1) // GLA_DV
    row_v = lax.broadcasted_iota(jnp.int32, (GLA_W, GLA_QK), 0) // GLA_DV
    col_k = lax.broadcasted_iota(jnp.int32, (GLA_W, GLA_QK), 1) // GLA_DK
    blockdiag = (row_v == col_k).astype(F32)
    rr = lax.broadcasted_iota(jnp.int32, (GLA_W, GLA_W), 0) // GLA_DV
    cc = lax.broadcasted_iota(jnp.int32, (GLA_W, GLA_W), 1) // GLA_DV
    avg = jnp.where(rr == cc, 1.0 / GLA_DV, 0.0).astype(BF16)

    for c in range(tb // CHUNK):
        rows = slice(c * CHUNK, (c + 1) * CHUNK)
        q = gqk_ref[rows, 0:GLA_QK] * (GLA_DK ** -0.5)
        k = gqk_ref[rows, GLA_QK:2 * GLA_QK]
        v = gv_ref[rows, :]
        z = _dot(sm_ref[rows, :], wa_ref[...]) + ba_ref[...]
        log_a = _log_sigmoid(z) / GLA_TAU
        cum = _cumsum_rows(log_a, tri)
        cum_end = cum[CHUNK - 1:CHUNK, :]
        q_dec = q * jnp.exp(cum)
        k_inv = k * jnp.exp(-cum)
        k_end = k * jnp.exp(cum_end - cum)
        st = st_sc[...]
        o = _dot_nt(q_dec, st)
        for h in range(GLA_HEADS):
            qh = jnp.where(lane_k == h, q_dec, 0.0)
            sc = jnp.where(causal, _dot_nt(qh, k_inv), 0.0)
            o = o + jnp.where(lane_v == h, _dot(sc, v), 0.0)
        st_sc[...] = st * jnp.exp(cum_end) + _dot_tn(v, k_end) * blockdiag
        o2 = o * o
        hi = o2.astype(BF16)
        lo = (o2 - hi.astype(F32)).astype(BF16)
        ms = jnp.dot(hi, avg, preferred_element_type=F32) + jnp.dot(lo, avg, preferred_element_type=F32)
        o = o * lax.rsqrt(ms + LN_EPS) * ng_ref[...]
        g = go_ref[rows, :]
        o_ref[rows, :] = (o * (g * _sigmoid(g))).astype(o_ref.dtype)

    sT_ref[...] = st_sc[...]


def _gla_call(gqk, gv, go, sm, wa_p, ba, ng4, s0_t):
    b, t, _ = gqk.shape
    tb = _row_tile(t, 256)
    row = lambda w: pl.BlockSpec((None, tb, w), lambda bi, i: (bi, i, 0))
    full = lambda a: pl.BlockSpec(a.shape, lambda bi, i: (0,) * a.ndim)
    st_spec = pl.BlockSpec((None, GLA_W, GLA_QK), lambda bi, i: (bi, 0, 0))
    return pl.pallas_call(
        functools.partial(_gla_kernel, tb=tb), grid=(b, t // tb),
        in_specs=[row(2 * GLA_QK), row(GLA_W), row(GLA_W), row(LANES), full(wa_p), full(ba), full(ng4), st_spec],
        out_specs=[row(GLA_W), st_spec],
        out_shape=[jax.ShapeDtypeStruct((b, t, GLA_W), BF16), jax.ShapeDtypeStruct((b, GLA_W, GLA_QK), F32)],
        scratch_shapes=[pltpu.VMEM((GLA_W, GLA_QK), F32)],
        compiler_params=_cparams(("parallel", "arbitrary")), name="gla")(gqk, gv, go, sm, wa_p, ba, ng4, s0_t)


def _cmul(ar, ai, br, bi):
    return ar * br - ai * bi, ar * bi + ai * br


def _shift_rows(x, d):
    rolled = pltpu.roll(x, d, 0)
    r = lax.broadcasted_iota(jnp.int32, x.shape, 0)
    return jnp.where(r >= d, rolled, 0.0)


def _s5_prep_kernel(lre_ref, lim_ref, ldt_ref, bre_ref, bim_ref, pre_ref, pim_ref, bbre_ref, bbim_ref):
    lam_re, lam_im = lre_ref[...], lim_ref[...]
    dt = jnp.exp(ldt_ref[...])
    mag = jnp.exp(lam_re * dt)
    ab_re = mag * jnp.cos(lam_im * dt)
    ab_im = mag * jnp.sin(lam_im * dt)
    den = lam_re * lam_re + lam_im * lam_im
    f_re = ((ab_re - 1.0) * lam_re + ab_im * lam_im) / den
    f_im = (ab_im * lam_re - (ab_re - 1.0) * lam_im) / den
    b_re, b_im = bre_ref[...], bim_ref[...]
    bbre_ref[...] = f_re * b_re - f_im * b_im
    bbim_ref[...] = f_re * b_im + f_im * b_re
    pr = jnp.broadcast_to(ab_re, (S5_TILE, S5_N))
    pi = jnp.broadcast_to(ab_im, (S5_TILE, S5_N))
    r = lax.broadcasted_iota(jnp.int32, (S5_TILE, S5_N), 0)
    d = 1
    while d < S5_TILE:
        sr = jnp.where(r >= d, pltpu.roll(pr, d, 0), 1.0)
        si = jnp.where(r >= d, pltpu.roll(pi, d, 0), 0.0)
        pr, pi = _cmul(pr, pi, sr, si)
        d *= 2
    pre_ref[...] = pr
    pim_ref[...] = pi


def _s5_prep_call(lam_re, lam_im, log_dt, b_re_t, b_im_t):
    sds = jax.ShapeDtypeStruct
    return pl.pallas_call(
        _s5_prep_kernel,
        out_shape=[sds((S5_TILE, S5_N), F32), sds((S5_TILE, S5_N), F32),
                   sds((S5_CH, S5_N), F32), sds((S5_CH, S5_N), F32)],
        compiler_params=pltpu.CompilerParams(vmem_limit_bytes=VMEM_LIMIT), name="s5_prep")(
            lam_re, lam_im, log_dt, b_re_t, b_im_t)


def _s5_kernel(u_ref, wbu_ref, pre_ref, pim_ref, wcr_ref, wci_ref, d_ref, wg_ref, bg_ref, h0r_ref, h0i_ref,
               o_ref, hr_ref, hi_ref, cr_sc, ci_sc, *, tt):
    i = pl.program_id(1)

    @pl.when(i == 0)
    def _():
        cr_sc[...] = h0r_ref[...]
        ci_sc[...] = h0i_ref[...]

    u = u_ref[...]
    bu = _dot(u, wbu_ref[...])
    xr, xi = bu[:, 0:S5_N], bu[:, S5_N:2 * S5_N]
    d = 1
    while d < tt:
        ar, ai = pre_ref[d - 1:d, :], pim_ref[d - 1:d, :]
        sr, si = _shift_rows(xr, d), _shift_rows(xi, d)
        mr, mi = _cmul(ar, ai, sr, si)
        xr, xi = xr + mr, xi + mi
        d *= 2
    cr, ci = cr_sc[...], ci_sc[...]
    mr, mi = _cmul(pre_ref[...], pim_ref[...], cr, ci)
    xr, xi = xr + mr, xi + mi
    cr_sc[...] = xr[tt - 1:tt, :]
    ci_sc[...] = xi[tt - 1:tt, :]
    hr_ref[...] = xr[tt - 1:tt, :]
    hi_ref[...] = xi[tt - 1:tt, :]
    y = _dot(xr, wcr_ref[...]) - _dot(xi, wci_ref[...]) + d_ref[...] * u
    z = _gelu_tanh(y)
    o_ref[...] = (z * _sigmoid(_dot(z, wg_ref[...]) + bg_ref[...])).astype(o_ref.dtype)


def _s5_call(u, wbu, pre, pim, wcr, wci, d_skip, wg, bg, h0r, h0i):
    b, t, _ = u.shape
    tt = _row_tile(t, S5_TILE)
    full = lambda a: pl.BlockSpec(a.shape, lambda bi, i: (0,) * a.ndim)
    pspec = pl.BlockSpec((tt, S5_N), lambda bi, i: (0, 0))
    hspec = pl.BlockSpec((None, 1, S5_N), lambda bi, i: (bi, 0, 0))
    row = pl.BlockSpec((None, tt, S5_W), lambda bi, i: (bi, i, 0))
    return pl.pallas_call(
        functools.partial(_s5_kernel, tt=tt), grid=(b, t // tt),
        in_specs=[row, full(wbu), pspec, pspec, full(wcr), full(wci), full(d_skip), full(wg), full(bg), hspec, hspec],
        out_specs=[row, hspec, hspec],
        out_shape=[jax.ShapeDtypeStruct((b, t, S5_W), BF16), jax.ShapeDtypeStruct((b, 1, S5_N), F32),
                   jax.ShapeDtypeStruct((b, 1, S5_N), F32)],
        scratch_shapes=[pltpu.VMEM((1, S5_N), F32), pltpu.VMEM((1, S5_N), F32)],
        compiler_params=_cparams(("parallel", "arbitrary")), name="s5")(
            u, wbu, pre, pim, wcr, wci, d_skip, wg, bg, h0r, h0i)


def _mix_ln_kernel(gla_ref, s5_ref, fox_ref, x_ref, w_ref, g_ref, b_ref, o_ref):
    mixed = jnp.dot(gla_ref[...], w_ref[0:GLA_W, :], preferred_element_type=F32)
    mixed = mixed + jnp.dot(s5_ref[...], w_ref[GLA_W:GLA_W + S5_W, :], preferred_element_type=F32)
    mixed = mixed + jnp.dot(fox_ref[...], w_ref[GLA_W + S5_W:, :], preferred_element_type=F32)
    o_ref[...] = _layer_norm(DN_ALPHA * x_ref[...] + mixed, g_ref[...], b_ref[...])


def _mix_ln_call(o_gla, o_s5, o_fox, x2, w, g, b):
    n = x2.shape[0]
    tm = _row_tile(n, 512)
    row = lambda w_: pl.BlockSpec((tm, w_), lambda i: (i, 0))
    full = lambda a: pl.BlockSpec(a.shape, lambda i: (0,) * a.ndim)
    return pl.pallas_call(
        _mix_ln_kernel, grid=(n // tm,),
        in_specs=[row(GLA_W), row(S5_W), row(FOX_W), row(D_MODEL), full(w), full(g), full(b)],
        out_specs=row(D_MODEL), out_shape=jax.ShapeDtypeStruct((n, D_MODEL), F32),
        compiler_params=_cparams(("parallel",)), name="mix_ln")(o_gla, o_s5, o_fox, x2, w, g, b)


def _mem_kv_kernel(m_ref, wk_ref, wv_ref, k_ref, v_ref):
    mb = m_ref[...].astype(BF16)
    k_ref[...] = jnp.dot(mb, wk_ref[...], preferred_element_type=F32)
    v_ref[...] = jnp.dot(mb, wv_ref[...], preferred_element_type=F32)


def _mem_kv_call(mem2, wk, wv):
    n = mem2.shape[0]
    tm = _row_tile(n, 256)
    row = pl.BlockSpec((tm, D_MODEL), lambda i: (i, 0))
    orow = pl.BlockSpec((tm, MEM_W), lambda i: (i, 0))
    full = lambda a: pl.BlockSpec(a.shape, lambda i: (0,) * a.ndim)
    sds = jax.ShapeDtypeStruct((n, MEM_W), F32)
    return pl.pallas_call(
        _mem_kv_kernel, grid=(n // tm,), in_specs=[row, full(wk), full(wv)], out_specs=[orow, orow],
        out_shape=[sds, sds], compiler_params=_cparams(("parallel",)), name="mem_kv")(mem2, wk, wv)


def _mem_attn_kernel(x_ref, wq_ref, mk_ref, mv_ref, wo_ref, g_ref, b_ref, o_ref):
    x = x_ref[...]
    q = jnp.dot(x.astype(BF16), wq_ref[...], preferred_element_type=F32).astype(BF16)
    mk = mk_ref[...].astype(BF16)
    mv = mv_ref[...].astype(BF16)
    heads = []
    for h in range(MEM_HEADS):
        sl = slice(h * MEM_HD, (h + 1) * MEM_HD)
        s = _dot_nt(q[:, sl], mk[:, sl]) * (MEM_HD ** -0.5)
        m = jnp.max(s, axis=1, keepdims=True)
        e = jnp.exp(s - m)
        p = e / jnp.sum(e, axis=1, keepdims=True)
        heads.append(_dot(p, mv[:, sl]).astype(BF16))
    o = jnp.concatenate(heads, axis=1)
    att = jnp.dot(o, wo_ref[...], preferred_element_type=F32)
    o_ref[...] = _layer_norm(DN_ALPHA * x + att, g_ref[...], b_ref[...])


def _mem_attn_call(x3, wq, mk, mv, wo, g, b):
    bsz, t, _ = x3.shape
    tm = _row_tile(t, 512)
    row = pl.BlockSpec((None, tm, D_MODEL), lambda bi, i: (bi, i, 0))
    mem = pl.BlockSpec((None, MEM_TOKENS, MEM_W), lambda bi, i: (bi, 0, 0))
    full = lambda a: pl.BlockSpec(a.shape, lambda bi, i: (0,) * a.ndim)
    return pl.pallas_call(
        _mem_attn_kernel, grid=(bsz, t // tm),
        in_specs=[row, full(wq), mem, mem, full(wo), full(g), full(b)], out_specs=row,
        out_shape=jax.ShapeDtypeStruct((bsz, t, D_MODEL), F32),
        compiler_params=_cparams(("parallel", "parallel")), name="mem_attn")(x3, wq, mk, mv, wo, g, b)


def _shift_with_halo(u, halo, d):
    rolled = pltpu.roll(u, d, 0)
    r8 = lax.broadcasted_iota(jnp.int32, halo.shape, 0)
    head = jnp.where(r8 < d, pltpu.roll(halo, d, 0), rolled[0:SUBLANES, :])
    return jnp.concatenate([head, rolled[SUBLANES:, :]], axis=0)


def _ffn_kernel(x_ref, xh_ref, wua_ref, wug_ref, pa_ref, pg_ref, cwa_ref, cwg_ref, cba_ref, cbg_ref, wd_ref,
                g_ref, b_ref, o_ref, ta_ref, tg_ref, acc_sc, *, tm):
    i = pl.program_id(1)
    j = pl.program_id(2)

    @pl.when(j == 0)
    def _():
        acc_sc[...] = jnp.zeros(acc_sc.shape, F32)

    xb = x_ref[...].astype(BF16)
    xhb = xh_ref[...].astype(BF16)

    def conv_branch(wu_ref, prev_ref, cw_ref, cb_ref, tail_ref):
        u = jnp.dot(xb, wu_ref[...], preferred_element_type=F32)
        uh = jnp.dot(xhb, wu_ref[...], preferred_element_type=F32)
        halo = jnp.where(i == 0, prev_ref[...], uh)
        tail_ref[...] = u[tm - SUBLANES:, :]
        y = cb_ref[...] + cw_ref[0:1, :] * _shift_with_halo(u, halo, 2)
        y = y + cw_ref[1:2, :] * _shift_with_halo(u, halo, 1)
        return y + cw_ref[2:3, :] * u

    a = conv_branch(wua_ref, pa_ref, cwa_ref, cba_ref, ta_ref)
    g = conv_branch(wug_ref, pg_ref, cwg_ref, cbg_ref, tg_ref)
    hidden = _gelu_tanh(a) * g
    acc_sc[...] += jnp.dot(hidden.astype(BF16), wd_ref[...], preferred_element_type=F32)

    @pl.when(j == FFN_NCH - 1)
    def _():
        o_ref[...] = _layer_norm(DN_ALPHA * x_ref[...] + acc_sc[...], g_ref[...], b_ref[...])


def _ffn_call(x3, w_up, conv_w, conv_b, w_down, prev8, g, b):
    bsz, t, _ = x3.shape
    tm = _row_tile(t, 512)
    c = FFN_CHUNK
    row = pl.BlockSpec((None, tm, D_MODEL), lambda bi, i, j: (bi, i, 0))
    halo = pl.BlockSpec((None, SUBLANES, D_MODEL),
                        lambda bi, i, j: (bi, jnp.maximum(i * (tm // SUBLANES) - 1, 0), 0))
    col_a = lambda r: pl.BlockSpec((r, c), lambda bi, i, j: (0, j))
    col_g = lambda r: pl.BlockSpec((r, c), lambda bi, i, j: (0, FFN_NCH + j))
    prev_a = pl.BlockSpec((None, SUBLANES, c), lambda bi, i, j: (bi, 0, j))
    prev_g = pl.BlockSpec((None, SUBLANES, c), lambda bi, i, j: (bi, 0, FFN_NCH + j))
    vec = pl.BlockSpec((1, D_MODEL), lambda bi, i, j: (0, 0))
    tail_spec = pl.BlockSpec((None, None, SUBLANES, c), lambda bi, i, j: (bi, i, 0, j))
    tail = jax.ShapeDtypeStruct((bsz, t // tm, SUBLANES, D_FF), F32)
    return pl.pallas_call(
        functools.partial(_ffn_kernel, tm=tm), grid=(bsz, t // tm, FFN_NCH),
        in_specs=[row, halo, col_a(D_MODEL), col_g(D_MODEL), prev_a, prev_g, col_a(CONV_W), col_g(CONV_W),
                  col_a(1), col_g(1), pl.BlockSpec((c, D_MODEL), lambda bi, i, j: (j, 0)), vec, vec],
        out_specs=[row, tail_spec, tail_spec],
        out_shape=[jax.ShapeDtypeStruct((bsz, t, D_MODEL), F32), tail, tail],
        scratch_shapes=[pltpu.VMEM((tm, D_MODEL), F32)],
        compiler_params=_cparams(("parallel", "arbitrary", "arbitrary")), name="conv_ffn")(
            x3, x3, w_up, w_up, prev8, prev8, conv_w, conv_w, conv_b, conv_b, w_down, g, b)


def _block_diag(blocks):
    g, r, c = blocks.shape
    eye = jnp.eye(g, dtype=blocks.dtype)
    return (blocks[:, :, None, :] * eye[:, None, :, None]).reshape(g * r, g * c)


def _prep_layer(l, p):
    w = p['w_in'][l]
    offs = [0, 128, 256, 512, 768, 784, 1040, 1552, 2064, 2576, 2584]
    gq, gk, gv, go, ga, su, fq, fk, fv, ff = [w[:, offs[n]:offs[n + 1]] for n in range(10)]
    pad = jnp.zeros((D_MODEL, LANES - FOX_HEADS - GLA_RANK), F32)
    w_r = jnp.concatenate([gq, gk, gv, go, su, fq, fk, fv, ff, ga, pad], axis=1).astype(BF16)
    pad_heads = lambda m: jnp.pad(m.reshape(D_MODEL, FOX_HEADS, FOX_HD), ((0, 0), (0, 0), (0, LANES - FOX_HD))
                                  ).reshape(D_MODEL, FOX_AUG_W)
    wqt_pad = pad_heads(fq).T.astype(BF16)
    wk_pad = pad_heads(fk).astype(BF16)
    wvt = fv.T.astype(BF16)
    wa_p = jnp.zeros((LANES, GLA_QK), F32).at[SM_GA:SM_GA + GLA_RANK].set(p['gla_w_a2'][l]).astype(BF16)
    row = lambda a: a.reshape(1, -1)
    flat = lambda a: a.reshape(1, S5_N)
    b_t = lambda a: jnp.transpose(a, (2, 0, 1)).reshape(S5_CH, S5_N)
    pre, pim, bb_re, bb_im = _s5_prep_call(flat(p['s5_lam_re'][l]), flat(p['s5_lam_im'][l]),
                                           flat(p['s5_log_dt'][l]), b_t(p['s5_b_re'][l]), b_t(p['s5_b_im'][l]))
    bd_in = lambda bb: _block_diag(jnp.transpose(bb.reshape(S5_CH, S5_GROUPS, S5_STATE), (1, 0, 2)))
    wbu = jnp.concatenate([bd_in(bb_re), bd_in(bb_im)], axis=1).astype(BF16)
    bd_out = lambda cm: _block_diag(jnp.transpose(cm, (0, 2, 1))).astype(BF16)
    return dict(
        w_r=w_r, wqt_pad=wqt_pad, wk_pad=wk_pad, wvt=wvt,
        b_f=row(p['fox_b_f'][l]), wa_p=wa_p, ba=row(p['gla_b_a'][l]),
        ng4=jnp.tile(p['gla_norm_g'][l], GLA_HEADS).reshape(1, GLA_W),
        wbu=wbu, pre=pre, pim=pim, wcr=bd_out(p['s5_c_re'][l]), wci=bd_out(p['s5_c_im'][l]),
        d_skip=row(p['s5_d'][l]), wg=p['s5_w_glu'][l].astype(BF16), bg=row(p['s5_b_glu'][l]),
        w_mix=p['w_mix_out'][l].astype(BF16), ln1_g=row(p['ln1_g'][l]), ln1_b=row(p['ln1_b'][l]),
        wq=p['mem_w_q'][l].astype(BF16), wk=p['mem_w_k'][l].astype(BF16), wv=p['mem_w_v'][l].astype(BF16),
        wo=p['mem_w_o'][l].astype(BF16), ln2_g=row(p['ln2_g'][l]), ln2_b=row(p['ln2_b'][l]),
        w_up=p['ffn_w_up'][l].astype(BF16), conv_w=p['ffn_conv_w'][l], conv_b=row(p['ffn_conv_b'][l]),
        w_down=p['ffn_w_down'][l].astype(BF16), ln3_g=row(p['ln3_g'][l]), ln3_b=row(p['ln3_b'][l]))


def _gla_state_in(s0):
    eye = jnp.eye(GLA_HEADS, dtype=s0.dtype)
    st = jnp.transpose(s0, (0, 1, 3, 2))
    full = st[:, :, :, None, :] * eye[None, :, None, :, None]
    return full.reshape(s0.shape[0], GLA_W, GLA_QK)


def _gla_state_out(st):
    s5 = st.reshape(st.shape[0], GLA_HEADS, GLA_DV, GLA_HEADS, GLA_DK)
    diag = jnp.stack([s5[:, h, :, h, :] for h in range(GLA_HEADS)], axis=1)
    return jnp.transpose(diag, (0, 1, 3, 2))


def _layer(x3, mem_k, mem_v, gla_s0, s5_h0_re, s5_h0_im, past_k, past_v, past_logf, conv_prev, lp):
    bsz, t, _ = x3.shape
    n = bsz * t
    x2 = x3.reshape(n, D_MODEL)
    gqk, gv, go, su, fk, fv, sm, logf3, *fox_ops = _in_proj_call(x3, lp, fox_t=past_k is None)

    o_gla, st_t = _gla_call(gqk, gv, go, sm, lp['wa_p'], lp['ba'], lp['ng4'], _gla_state_in(gla_s0))
    o_s5, h_re, h_im = _s5_call(su, lp['wbu'], lp['pre'], lp['pim'], lp['wcr'], lp['wci'], lp['d_skip'],
                                lp['wg'], lp['bg'], s5_h0_re.reshape(bsz, 1, S5_N), s5_h0_im.reshape(bsz, 1, S5_N))

    if past_k is None:
        qt_pad, k_pad, vt = fox_ops
        k_aug, qt_aug = _fox_aug_call(logf3, k_pad, qt_pad)
        o_fox = _flash_t_call(qt_aug, k_aug, vt)
    else:
        past = past_k.shape[1]
        k_all = jnp.concatenate([past_k.reshape(bsz, past, FOX_W), fk], axis=1)
        v_all = jnp.concatenate([past_v.reshape(bsz, past, FOX_W), fv], axis=1)
        cum_f = _cumsum_call(jnp.concatenate([past_logf, logf3], axis=1))
        o_fox = _flash_call(fox_ops[0], k_all, v_all, cum_f[:, past:], jnp.swapaxes(cum_f, 1, 2), past)

    x1 = _mix_ln_call(o_gla.reshape(n, GLA_W), o_s5.reshape(n, S5_W), o_fox.reshape(n, FOX_W), x2,
                      lp['w_mix'], lp['ln1_g'], lp['ln1_b'])
    x2b = _mem_attn_call(x1.reshape(bsz, t, D_MODEL), lp['wq'], mem_k, mem_v, lp['wo'], lp['ln2_g'], lp['ln2_b'])
    prev8 = jnp.concatenate([jnp.zeros((bsz, SUBLANES - (CONV_W - 1), 2 * D_FF), F32), conv_prev], axis=1)
    y3, tail_a, tail_g = _ffn_call(x2b, lp['w_up'], lp['conv_w'], lp['conv_b'], lp['w_down'], prev8,
                                   lp['ln3_g'], lp['ln3_b'])
    conv_new = jnp.concatenate([tail_a[:, -1], tail_g[:, -1]], axis=2)[:, SUBLANES - (CONV_W - 1):, :]
    states = (_gla_state_out(st_t), h_re.reshape(bsz, S5_GROUPS, S5_STATE), h_im.reshape(bsz, S5_GROUPS, S5_STATE),
              fk.reshape(bsz, t, FOX_HEADS, FOX_HD), fv.reshape(bsz, t, FOX_HEADS, FOX_HD), logf3, conv_new)
    return y3, states


def kernel(x_prompt, x_sample, mem_prompt, state_gla, state_s5_re, state_s5_im, cache_fox_k, cache_fox_v,
           cache_fox_logf, cache_mem_k, cache_mem_v, state_ffn_conv, ln_in_g, ln_in_b, w_in, gla_w_a2, gla_b_a,
           gla_norm_g, s5_lam_re, s5_lam_im, s5_log_dt, s5_b_re, s5_b_im, s5_c_re, s5_c_im, s5_d, s5_w_glu,
           s5_b_glu, fox_b_f, w_mix_out, ln1_g, ln1_b, mem_w_q, mem_w_k, mem_w_v, mem_w_o, ln2_g, ln2_b,
           ffn_w_up, ffn_conv_w, ffn_conv_b, ffn_w_down, ln3_g, ln3_b):
    params = dict(w_in=w_in, gla_w_a2=gla_w_a2, gla_b_a=gla_b_a, gla_norm_g=gla_norm_g, s5_lam_re=s5_lam_re,
                  s5_lam_im=s5_lam_im, s5_log_dt=s5_log_dt, s5_b_re=s5_b_re, s5_b_im=s5_b_im, s5_c_re=s5_c_re,
                  s5_c_im=s5_c_im, s5_d=s5_d, s5_w_glu=s5_w_glu, s5_b_glu=s5_b_glu, fox_b_f=fox_b_f,
                  w_mix_out=w_mix_out, ln1_g=ln1_g, ln1_b=ln1_b, mem_w_q=mem_w_q, mem_w_k=mem_w_k,
                  mem_w_v=mem_w_v, mem_w_o=mem_w_o, ln2_g=ln2_g, ln2_b=ln2_b, ffn_w_up=ffn_w_up,
                  ffn_conv_w=ffn_conv_w, ffn_conv_b=ffn_conv_b, ffn_w_down=ffn_w_down, ln3_g=ln3_g, ln3_b=ln3_b)
    bp, tp, _ = x_prompt.shape
    bs, ts, _ = x_sample.shape
    g_in, b_in = ln_in_g.reshape(1, D_MODEL), ln_in_b.reshape(1, D_MODEL)
    hp = _ln_call(x_prompt.reshape(bp * tp, D_MODEL), g_in, b_in).reshape(bp, tp, D_MODEL)
    hs = _ln_call(x_sample.reshape(bs * ts, D_MODEL), g_in, b_in).reshape(bs, ts, D_MODEL)
    mem2 = mem_prompt.reshape(bp * MEM_TOKENS, D_MODEL)
    zero_gla = jnp.zeros((bp, GLA_HEADS, GLA_DK, GLA_DV), F32)
    zero_s5 = jnp.zeros((bp, S5_GROUPS, S5_STATE), F32)
    zero_conv = jnp.zeros((bp, CONV_W - 1, 2 * D_FF), F32)
    prompt_states, sample_states = [], []
    for l in range(DEPTH):
        lp = _prep_layer(l, params)
        mk2, mv2 = _mem_kv_call(mem2, lp['wk'], lp['wv'])
        mk_p = mk2.reshape(bp, MEM_TOKENS, MEM_W)
        mv_p = mv2.reshape(bp, MEM_TOKENS, MEM_W)
        hp, st_p = _layer(hp, mk_p, mv_p, zero_gla, zero_s5, zero_s5, None, None, None, zero_conv, lp)
        prompt_states.append(st_p + (mk_p.reshape(bp, MEM_TOKENS, MEM_HEADS, MEM_HD),
                                     mv_p.reshape(bp, MEM_TOKENS, MEM_HEADS, MEM_HD)))
        hs, st_s = _layer(hs, cache_mem_k[l].reshape(bs, MEM_TOKENS, MEM_W),
                          cache_mem_v[l].reshape(bs, MEM_TOKENS, MEM_W), state_gla[l], state_s5_re[l],
                          state_s5_im[l], cache_fox_k[l], cache_fox_v[l], cache_fox_logf[l], state_ffn_conv[l], lp)
        sample_states.append(st_s)
    p_out = [jnp.stack(z) for z in zip(*prompt_states)]
    s_out = [jnp.stack(z) for z in zip(*sample_states)]
    return (hp, hs, *p_out, *s_out)
```

```python
import functools
import math

import jax
import jax.numpy as jnp
from jax import lax
from jax.experimental import pallas as pl
from jax.experimental.pallas import tpu as pltpu

F32 = jnp.float32
BF16 = jnp.bfloat16

D_MODEL = 1024
DEPTH = 2
CHUNK = 64
GLA_HEADS = 4
GLA_DK = 32
GLA_DV = 64
GLA_RANK = 16
GLA_TAU = 16.0
S5_GROUPS = 16
S5_CH = 16
S5_STATE = 64
FOX_HEADS = 8
FOX_HD = 64
MEM_TOKENS = 256
MEM_HEADS = 4
MEM_HD = 256
D_FF = 2816
CONV_W = 3
LN_EPS = 1e-5
DN_ALPHA = (2 * DEPTH) ** 0.25

GLA_QK = GLA_HEADS * GLA_DK
GLA_W = GLA_HEADS * GLA_DV
S5_W = S5_GROUPS * S5_CH
S5_N = S5_GROUPS * S5_STATE
FOX_W = FOX_HEADS * FOX_HD
MEM_W = MEM_HEADS * MEM_HD

LANES = 128
SUBLANES = 8
VMEM_LIMIT = 56 * 1024 * 1024

C_GQK = 0
C_GV = 256
C_GO = 512
C_SU = 768
C_FQ = 1024
C_FK = 1536
C_FV = 2048
C_SM = 2560
N_INR = 2688
SM_FF = 0
SM_GA = FOX_HEADS

FFN_CHUNK = 256
FFN_NCH = D_FF // FFN_CHUNK
FFN_TM = 256
S5_TILE = 256
FOX_SCALE = FOX_HD ** -0.5
LOG2E = math.log2(math.e)
FOX_AUG_W = FOX_HEADS * LANES
FOX_NPIECE = 3
FOX_TQ = 512
FOX_TK = 512


def _cparams(sem):
    return pltpu.CompilerParams(dimension_semantics=sem, vmem_limit_bytes=VMEM_LIMIT)


def _dot(a, b):
    return jnp.dot(a.astype(BF16), b.astype(BF16), preferred_element_type=F32)


def _dot_nt(a, b):
    return lax.dot_general(a.astype(BF16), b.astype(BF16), (((1,), (1,)), ((), ())),
                           preferred_element_type=F32)


def _dot_tn(a, b):
    return lax.dot_general(a.astype(BF16), b.astype(BF16), (((0,), (0,)), ((), ())),
                           preferred_element_type=F32)


def _split3(x):
    hi = x.astype(BF16)
    r1 = x - hi.astype(F32)
    mid = r1.astype(BF16)
    lo = (r1 - mid.astype(F32)).astype(BF16)
    return hi, mid, lo


def _tri(n):
    r = lax.broadcasted_iota(jnp.int32, (n, n), 0)
    c = lax.broadcasted_iota(jnp.int32, (n, n), 1)
    return (c <= r).astype(BF16)


def _cumsum_rows(x, tri):
    hi, mid, lo = _split3(x)
    d = lambda p: jnp.dot(tri, p, preferred_element_type=F32)
    return d(hi) + (d(mid) + d(lo))


def _layer_norm(x, g, b):
    mu = jnp.mean(x, axis=-1, keepdims=True)
    xc = x - mu
    var = jnp.mean(xc * xc, axis=-1, keepdims=True)
    return xc * lax.rsqrt(var + LN_EPS) * g + b


def _log_sigmoid(x):
    return jnp.minimum(x, 0.0) - jnp.log1p(jnp.exp(-jnp.abs(x)))


def _sigmoid(x):
    return 1.0 / (1.0 + jnp.exp(-x))


def _gelu_tanh(x):
    c = math.sqrt(2.0 / math.pi)
    return 0.5 * x * (1.0 + jnp.tanh(c * (x + 0.044715 * (x * x * x))))


def _row_tile(n, want):
    t = min(n, want)
    assert n % t == 0
    return t


def _ln_kernel(x_ref, g_ref, b_ref, o_ref):
    o_ref[...] = _layer_norm(x_ref[...], g_ref[...], b_ref[...])


def _ln_call(x2, g, b):
    n = x2.shape[0]
    tm = _row_tile(n, 512)
    row = pl.BlockSpec((tm, D_MODEL), lambda i: (i, 0))
    vec = pl.BlockSpec((1, D_MODEL), lambda i: (0, 0))
    return pl.pallas_call(
        _ln_kernel, grid=(n // tm,), in_specs=[row, vec, vec], out_specs=row,
        out_shape=jax.ShapeDtypeStruct((n, D_MODEL), F32),
        compiler_params=_cparams(("parallel",)), name="ln_in")(x2, g, b)


def _in_proj_kernel(x_ref, w_ref, bf_ref, gqk_ref, gv_ref, go_ref, su_ref, fk_ref, fv_ref, sm_ref, logf_ref,
                    fq_ref):
    xb = x_ref[...].astype(BF16)
    proj = lambda lo, hi: jnp.dot(xb, w_ref[:, lo:hi], preferred_element_type=F32)
    gqk_ref[...] = proj(C_GQK, C_GV)
    gv_ref[...] = proj(C_GV, C_GO)
    go_ref[...] = proj(C_GO, C_SU)
    su_ref[...] = proj(C_SU, C_FQ)
    fk_ref[...] = proj(C_FK, C_FV)
    fv_ref[...] = proj(C_FV, C_SM)
    sm = proj(C_SM, N_INR)
    sm_ref[...] = sm
    logf_ref[...] = _log_sigmoid(sm[:, SM_FF:SM_FF + FOX_HEADS] + bf_ref[...])
    fq_ref[...] = (proj(C_FQ, C_FK) * FOX_SCALE).astype(BF16)


def _in_proj_t_kernel(x_ref, w_ref, bf_ref, wqt_ref, wkp_ref, wvt_ref, gqk_ref, gv_ref, go_ref, su_ref, fk_ref,
                      fv_ref, sm_ref, logf_ref, qt_ref, kp_ref, vt_ref):
    xb = x_ref[...].astype(BF16)
    proj = lambda lo, hi: jnp.dot(xb, w_ref[:, lo:hi], preferred_element_type=F32)
    gqk_ref[...] = proj(C_GQK, C_GV)
    gv_ref[...] = proj(C_GV, C_GO)
    go_ref[...] = proj(C_GO, C_SU)
    su_ref[...] = proj(C_SU, C_FQ)
    fk_ref[...] = proj(C_FK, C_FV)
    fv_ref[...] = proj(C_FV, C_SM)
    sm = proj(C_SM, N_INR)
    sm_ref[...] = sm
    logf_ref[...] = _log_sigmoid(sm[:, SM_FF:SM_FF + FOX_HEADS] + bf_ref[...])
    qt_ref[...] = (_dot_nt(wqt_ref[...], xb) * (FOX_SCALE * LOG2E)).astype(BF16)
    kp_ref[...] = jnp.dot(xb, wkp_ref[...], preferred_element_type=F32).astype(BF16)
    vt_ref[...] = _dot_nt(wvt_ref[...], xb).astype(BF16)


def _in_proj_call(x3, lp, fox_t):
    bsz, t, _ = x3.shape
    tm = _row_tile(t, 512)
    row = lambda w: pl.BlockSpec((None, tm, w), lambda bi, i: (bi, i, 0))
    col = lambda r: pl.BlockSpec((None, r, tm), lambda bi, i: (bi, 0, i))
    full = lambda a: pl.BlockSpec(a.shape, lambda bi, i: (0,) * a.ndim)
    sds = lambda w, dt: jax.ShapeDtypeStruct((bsz, t, w), dt)
    in_arrays = [x3, lp['w_r'], lp['b_f']]
    out_specs = [row(256), row(256), row(256), row(256), row(FOX_W), row(FOX_W), row(LANES), row(FOX_HEADS)]
    out_shape = [sds(256, F32), sds(256, F32), sds(256, F32), sds(256, F32), sds(FOX_W, F32), sds(FOX_W, F32),
                 sds(LANES, F32), sds(FOX_HEADS, F32)]
    if fox_t:
        body = _in_proj_t_kernel
        in_arrays += [lp['wqt_pad'], lp['wk_pad'], lp['wvt']]
        out_specs += [col(FOX_AUG_W), row(FOX_AUG_W), col(FOX_W)]
        out_shape += [jax.ShapeDtypeStruct((bsz, FOX_AUG_W, t), BF16), sds(FOX_AUG_W, BF16),
                      jax.ShapeDtypeStruct((bsz, FOX_W, t), BF16)]
    else:
        body = _in_proj_kernel
        out_specs += [row(FOX_W)]
        out_shape += [sds(FOX_W, BF16)]
    in_specs = [row(D_MODEL)] + [full(a) for a in in_arrays[1:]]
    return pl.pallas_call(
        body, grid=(bsz, t // tm), in_specs=in_specs, out_specs=out_specs, out_shape=out_shape,
        compiler_params=_cparams(("parallel", "parallel")), name="in_proj")(*in_arrays)


def _cumsum_kernel(x_ref, o_ref, *, t, blk):
    tri = _tri(blk)
    carry = jnp.zeros((1, FOX_HEADS), F32)
    for c in range(t // blk):
        cs = _cumsum_rows(x_ref[c * blk:(c + 1) * blk, :], tri) + carry
        o_ref[c * blk:(c + 1) * blk, :] = cs
        carry = cs[blk - 1:blk, :]


def _cumsum_call(x3):
    b, t, h = x3.shape
    blk = max(c for c in range(SUBLANES, 1024 + 1, SUBLANES) if t % c == 0)
    spec = pl.BlockSpec((None, t, h), lambda i: (i, 0, 0))
    return pl.pallas_call(
        functools.partial(_cumsum_kernel, t=t, blk=blk), grid=(b,), in_specs=[spec], out_specs=spec,
        out_shape=jax.ShapeDtypeStruct((b, t, h), F32),
        compiler_params=_cparams(("parallel",)), name="fox_cumsum")(x3)


def _fox_aug_kernel(logf_ref, kp_ref, qt_ref, ko_ref, qo_ref, carry_sc, *, tt):
    @pl.when(pl.program_id(1) == 0)
    def _():
        carry_sc[...] = jnp.zeros(carry_sc.shape, F32)

    cum = _cumsum_rows(logf_ref[...], _tri(tt)) + carry_sc[...]
    carry_sc[...] = cum[tt - 1:tt, :]
    pieces = [p.astype(F32) for p in _split3(cum * LOG2E)]

    lane = lax.broadcasted_iota(jnp.int32, (FOX_HEADS, FOX_AUG_W), 1)
    base_k = lax.broadcasted_iota(jnp.int32, (FOX_HEADS, FOX_AUG_W), 0) * LANES + FOX_HD + FOX_NPIECE
    aug_k = jnp.zeros((tt, FOX_AUG_W), F32)
    for p in range(FOX_NPIECE):
        aug_k = aug_k - jnp.dot(pieces[p], (lane == base_k + p).astype(F32), preferred_element_type=F32)
    off_k = lax.broadcasted_iota(jnp.int32, (1, FOX_AUG_W), 1) & (LANES - 1)
    ones_k = jnp.logical_and(off_k >= FOX_HD, off_k < FOX_HD + FOX_NPIECE).astype(F32)
    ko_ref[...] = (kp_ref[...].astype(F32) + aug_k + ones_k).astype(BF16)

    row = lax.broadcasted_iota(jnp.int32, (FOX_AUG_W, FOX_HEADS), 0)
    base_q = lax.broadcasted_iota(jnp.int32, (FOX_AUG_W, FOX_HEADS), 1) * LANES + FOX_HD
    aug_q = jnp.zeros((FOX_AUG_W, tt), F32)
    for p in range(FOX_NPIECE):
        aug_q = aug_q + lax.dot_general((row == base_q + p).astype(F32), pieces[p], (((1,), (1,)), ((), ())),
                                        preferred_element_type=F32)
    off_q = lax.broadcasted_iota(jnp.int32, (FOX_AUG_W, 1), 0) & (LANES - 1)
    ones_q = jnp.logical_and(off_q >= FOX_HD + FOX_NPIECE, off_q < FOX_HD + 2 * FOX_NPIECE).astype(F32)
    qo_ref[...] = (qt_ref[...].astype(F32) + aug_q + ones_q).astype(BF16)


def _fox_aug_call(logf3, k_pad, qt_pad):
    bsz, t, _ = logf3.shape
    tt = _row_tile(t, 512)
    kspec = pl.BlockSpec((None, tt, FOX_AUG_W), lambda bi, i: (bi, i, 0))
    qspec = pl.BlockSpec((None, FOX_AUG_W, tt), lambda bi, i: (bi, 0, i))
    return pl.pallas_call(
        functools.partial(_fox_aug_kernel, tt=tt), grid=(bsz, t // tt),
        in_specs=[pl.BlockSpec((None, tt, FOX_HEADS), lambda bi, i: (bi, i, 0)), kspec, qspec],
        out_specs=[kspec, qspec],
        out_shape=[jax.ShapeDtypeStruct(k_pad.shape, BF16), jax.ShapeDtypeStruct(qt_pad.shape, BF16)],
        scratch_shapes=[pltpu.VMEM((1, FOX_HEADS), F32)],
        input_output_aliases={1: 0, 2: 1},
        compiler_params=_cparams(("parallel", "arbitrary")), name="fox_aug")(logf3, k_pad, qt_pad)


def _flash_t_kernel(qi_ref, kj_ref, last_ref, qt_ref, k_ref, vt_ref, o_ref, m_sc, l_sc, acc_sc, *, tq, tk):
    step = pl.program_id(1)
    i = qi_ref[step]
    j = kj_ref[step]

    @pl.when(j == 0)
    def _():
        m_sc[...] = jnp.full(m_sc.shape, -jnp.inf, F32)
        l_sc[...] = jnp.zeros(l_sc.shape, F32)
        acc_sc[...] = jnp.zeros(acc_sc.shape, F32)

    def compute(masked):
        if masked:
            kpos = j * tk + lax.broadcasted_iota(jnp.int32, (tk, tq), 0)
            qpos = i * tq + lax.broadcasted_iota(jnp.int32, (tk, tq), 1)
            visible = kpos <= qpos
        for h in range(FOX_HEADS):
            s_t = jnp.dot(k_ref[:, h * LANES:(h + 1) * LANES], qt_ref[h * LANES:(h + 1) * LANES, :],
                          preferred_element_type=F32)
            if masked:
                s_t = jnp.where(visible, s_t, -jnp.inf)
            m_prev = m_sc[h:h + 1, :]
            m_new = jnp.maximum(m_prev, jnp.max(s_t, axis=0, keepdims=True))
            alpha = jnp.exp2(m_prev - m_new)
            p_t = jnp.exp2(s_t - m_new)
            l_sc[h:h + 1, :] = alpha * l_sc[h:h + 1, :] + jnp.sum(p_t, axis=0, keepdims=True)
            rows = slice(h * FOX_HD, (h + 1) * FOX_HD)
            acc_sc[rows, :] = acc_sc[rows, :] * alpha + jnp.dot(vt_ref[rows, :], p_t.astype(BF16),
                                                                preferred_element_type=F32)
            m_sc[h:h + 1, :] = m_new

    fully_visible = j * tk + (tk - 1) <= i * tq

    @pl.when(fully_visible)
    def _():
        compute(False)

    @pl.when(jnp.logical_not(fully_visible))
    def _():
        compute(True)

    @pl.when(last_ref[step] == 1)
    def _():
        for h in range(FOX_HEADS):
            rows = slice(h * FOX_HD, (h + 1) * FOX_HD)
            acc_sc[rows, :] = acc_sc[rows, :] / l_sc[h:h + 1, :]
        o_ref[...] = acc_sc[...].T.astype(o_ref.dtype)


def _flash_t_call(qt_aug, k_aug, vt):
    bsz, _, t = qt_aug.shape
    tq, tk = _row_tile(t, FOX_TQ), _row_tile(t, FOX_TK)
    pairs = [(i, j) for i in range(t // tq) for j in range((i * tq + tq - 1) // tk + 1)]
    qi = jnp.asarray([p[0] for p in pairs], jnp.int32)
    kj = jnp.asarray([p[1] for p in pairs], jnp.int32)
    last = jnp.asarray([int(n + 1 == len(pairs) or pairs[n + 1][0] != p[0]) for n, p in enumerate(pairs)],
                       jnp.int32)
    grid_spec = pltpu.PrefetchScalarGridSpec(
        num_scalar_prefetch=3, grid=(bsz, len(pairs)),
        in_specs=[pl.BlockSpec((None, FOX_AUG_W, tq), lambda bi, s, qi, kj, last: (bi, 0, qi[s])),
                  pl.BlockSpec((None, tk, FOX_AUG_W), lambda bi, s, qi, kj, last: (bi, kj[s], 0)),
                  pl.BlockSpec((None, FOX_W, tk), lambda bi, s, qi, kj, last: (bi, 0, kj[s]))],
        out_specs=pl.BlockSpec((None, tq, FOX_W), lambda bi, s, qi, kj, last: (bi, qi[s], 0)),
        scratch_shapes=[pltpu.VMEM((FOX_HEADS, tq), F32), pltpu.VMEM((FOX_HEADS, tq), F32),
                        pltpu.VMEM((FOX_W, tq), F32)])
    return pl.pallas_call(
        functools.partial(_flash_t_kernel, tq=tq, tk=tk), grid_spec=grid_spec,
        out_shape=jax.ShapeDtypeStruct((bsz, t, FOX_W), BF16),
        compiler_params=_cparams(("parallel", "arbitrary")), name="fox_flash_t")(qi, kj, last, qt_aug, k_aug, vt)


def _flash_kernel(q_ref, k_ref, v_ref, fq_ref, fk_ref, o_ref, m_sc, l_sc, acc_sc, *, tq, tk, past, nkv):
    i = pl.program_id(1)
    j = pl.program_id(2)
    q_lo = past + i * tq

    @pl.when(j == 0)
    def _():
        m_sc[...] = jnp.full(m_sc.shape, -jnp.inf, F32)
        l_sc[...] = jnp.zeros(l_sc.shape, F32)
        acc_sc[...] = jnp.zeros(acc_sc.shape, F32)

    def compute(masked):
        if masked:
            qpos = q_lo + lax.broadcasted_iota(jnp.int32, (tq, tk), 0)
            kpos = j * tk + lax.broadcasted_iota(jnp.int32, (tq, tk), 1)
            visible = kpos <= qpos
        for h in range(FOX_HEADS):
            sl = slice(h * FOX_HD, (h + 1) * FOX_HD)
            s = _dot_nt(q_ref[:, sl], k_ref[:, sl])
            s = s + fq_ref[:, h:h + 1] - fk_ref[h:h + 1, :]
            if masked:
                s = jnp.where(visible, s, -jnp.inf)
            m_prev = m_sc[h]
            m_new = jnp.maximum(m_prev, jnp.max(s, axis=1, keepdims=True))
            alpha = jnp.exp(m_prev - m_new)
            p = jnp.exp(s - m_new[:, 0:1])
            l_sc[h] = alpha * l_sc[h] + jnp.sum(p, axis=1, keepdims=True)
            acc_sc[:, sl] = acc_sc[:, sl] * alpha[:, 0:FOX_HD] + _dot(p, v_ref[:, sl])
            m_sc[h] = m_new

    k_hi = j * tk + (tk - 1)
    needed = j * tk <= q_lo + (tq - 1)
    fully_visible = k_hi <= q_lo

    @pl.when(jnp.logical_and(needed, fully_visible))
    def _():
        compute(False)

    @pl.when(jnp.logical_and(needed, jnp.logical_not(fully_visible)))
    def _():
        compute(True)

    @pl.when(j == nkv - 1)
    def _():
        for h in range(FOX_HEADS):
            sl = slice(h * FOX_HD, (h + 1) * FOX_HD)
            o_ref[:, sl] = (acc_sc[:, sl] / l_sc[h][:, 0:FOX_HD]).astype(o_ref.dtype)


def _flash_call(q, k, v, cf_q, cf_k_t, past):
    b, tq_all, _ = q.shape
    tk_all = k.shape[1]
    tq = _row_tile(tq_all, 512)
    tk = tk_all if tk_all % 512 else 512
    nq, nkv = tq_all // tq, tk_all // tk

    def kv_idx(bi, i, j):
        last = (past + i * tq + tq - 1) // tk
        return (bi, jnp.minimum(j, last), 0)

    def fk_idx(bi, i, j):
        last = (past + i * tq + tq - 1) // tk
        return (bi, 0, jnp.minimum(j, last))

    qspec = pl.BlockSpec((None, tq, FOX_W), lambda bi, i, j: (bi, i, 0))
    return pl.pallas_call(
        functools.partial(_flash_kernel, tq=tq, tk=tk, past=past, nkv=nkv),
        grid=(b, nq, nkv),
        in_specs=[qspec,
                  pl.BlockSpec((None, tk, FOX_W), kv_idx),
                  pl.BlockSpec((None, tk, FOX_W), kv_idx),
                  pl.BlockSpec((None, tq, FOX_HEADS), lambda bi, i, j: (bi, i, 0)),
                  pl.BlockSpec((None, FOX_HEADS, tk), fk_idx)],
        out_specs=qspec,
        out_shape=jax.ShapeDtypeStruct((b, tq_all, FOX_W), BF16),
        scratch_shapes=[pltpu.VMEM((FOX_HEADS, tq, LANES), F32),
                        pltpu.VMEM((FOX_HEADS, tq, LANES), F32),
                        pltpu.VMEM((tq, FOX_W), F32)],
        compiler_params=_cparams(("parallel", "parallel", "arbitrary")), name="fox_flash")(
            q, k, v, cf_q, cf_k_t)


def _gla_kernel(gqk_ref, gv_ref, go_ref, sm_ref, wa_ref, ba_ref, ng_ref, s0_ref, o_ref, sT_ref, st_sc, *, tb):
    i = pl.program_id(1)

    @pl.when(i == 0)
    def _():
        st_sc[...] = s0_ref[...]

    tri = _tri(CHUNK)
    r_i = lax.broadcasted_iota(jnp.int32, (CHUNK, CHUNK), 0)
    c_i = lax.broadcasted_iota(jnp.int32, (CHUNK, CHUNK), 1)
    causal = c_i <= r_i
    lane_k = lax.broadcasted_iota(jnp.int32, (1, GLA_QK), 1) // GLA_DK
    lane_v = lax.broadcasted_iota(jnp.int32, (1, GLA_W), 1) // GLA_DV
    row_v = lax.broadcasted_iota(jnp.int32, (GLA_W, GLA_QK), 0) // GLA_DV
    col_k = lax.broadcasted_iota(jnp.int32, (GLA_W, GLA_QK), 1) // GLA_DK
    blockdiag = (row_v == col_k).astype(F32)
    rr = lax.broadcasted_iota(jnp.int32, (GLA_W, GLA_W), 0) // GLA_DV
    cc = lax.broadcasted_iota(jnp.int32, (GLA_W, GLA_W), 1) // GLA_DV
    avg = jnp.where(rr == cc, 1.0 / GLA_DV, 0.0).astype(BF16)

    for c in range(tb // CHUNK):
        rows = slice(c * CHUNK, (c + 1) * CHUNK)
        q = gqk_ref[rows, 0:GLA_QK] * (GLA_DK ** -0.5)
        k = gqk_ref[rows, GLA_QK:2 * GLA_QK]
        v = gv_ref[rows, :]
        z = _dot(sm_ref[rows, :], wa_ref[...]) + ba_ref[...]
        log_a = _log_sigmoid(z) / GLA_TAU
        cum = _cumsum_rows(log_a, tri)
        cum_end = cum[CHUNK - 1:CHUNK, :]
        q_dec = q * jnp.exp(cum)
        k_inv = k * jnp.exp(-cum)
        k_end = k * jnp.exp(cum_end - cum)
        st = st_sc[...]
        o = _dot_nt(q_dec, st)
        for h in range(GLA_HEADS):
            qh = jnp.where(lane_k == h, q_dec, 0.0)
            sc = jnp.where(causal, _dot_nt(qh, k_inv), 0.0)
            o = o + jnp.where(lane_v == h, _dot(sc, v), 0.0)
        st_sc[...] = st * jnp.exp(cum_end) + _dot_tn(v, k_end) * blockdiag
        o2 = o * o
        hi = o2.astype(BF16)
        lo = (o2 - hi.astype(F32)).astype(BF16)
        ms = jnp.dot(hi, avg, preferred_element_type=F32) + jnp.dot(lo, avg, preferred_element_type=F32)
        o = o * lax.rsqrt(ms + LN_EPS) * ng_ref[...]
        g = go_ref[rows, :]
        o_ref[rows, :] = (o * (g * _sigmoid(g))).astype(o_ref.dtype)

    sT_ref[...] = st_sc[...]


def _gla_call(gqk, gv, go, sm, wa_p, ba, ng4, s0_t):
    b, t, _ = gqk.shape
    tb = _row_tile(t, 256)
    row = lambda w: pl.BlockSpec((None, tb, w), lambda bi, i: (bi, i, 0))
    full = lambda a: pl.BlockSpec(a.shape, lambda bi, i: (0,) * a.ndim)
    st_spec = pl.BlockSpec((None, GLA_W, GLA_QK), lambda bi, i: (bi, 0, 0))
    return pl.pallas_call(
        functools.partial(_gla_kernel, tb=tb), grid=(b, t // tb),
        in_specs=[row(2 * GLA_QK), row(GLA_W), row(GLA_W), row(LANES), full(wa_p), full(ba), full(ng4), st_spec],
        out_specs=[row(GLA_W), st_spec],
        out_shape=[jax.ShapeDtypeStruct((b, t, GLA_W), BF16), jax.ShapeDtypeStruct((b, GLA_W, GLA_QK), F32)],
        scratch_shapes=[pltpu.VMEM((GLA_W, GLA_QK), F32)],
        compiler_params=_cparams(("parallel", "arbitrary")), name="gla")(gqk, gv, go, sm, wa_p, ba, ng4, s0_t)


def _cmul(ar, ai, br, bi):
    return ar * br - ai * bi, ar * bi + ai * br


def _shift_rows(x, d):
    rolled = pltpu.roll(x, d, 0)
    r = lax.broadcasted_iota(jnp.int32, x.shape, 0)
    return jnp.where(r >= d, rolled, 0.0)


def _s5_prep_kernel(lre_ref, lim_ref, ldt_ref, bre_ref, bim_ref, pre_ref, pim_ref, bbre_ref, bbim_ref):
    lam_re, lam_im = lre_ref[...], lim_ref[...]
    dt = jnp.exp(ldt_ref[...])
    mag = jnp.exp(lam_re * dt)
    ab_re = mag * jnp.cos(lam_im * dt)
    ab_im = mag * jnp.sin(lam_im * dt)
    den = lam_re * lam_re + lam_im * lam_im
    f_re = ((ab_re - 1.0) * lam_re + ab_im * lam_im) / den
    f_im = (ab_im * lam_re - (ab_re - 1.0) * lam_im) / den
    b_re, b_im = bre_ref[...], bim_ref[...]
    bbre_ref[...] = f_re * b_re - f_im * b_im
    bbim_ref[...] = f_re * b_im + f_im * b_re
    pr = jnp.broadcast_to(ab_re, (S5_TILE, S5_N))
    pi = jnp.broadcast_to(ab_im, (S5_TILE, S5_N))
    r = lax.broadcasted_iota(jnp.int32, (S5_TILE, S5_N), 0)
    d = 1
    while d < S5_TILE:
        sr = jnp.where(r >= d, pltpu.roll(pr, d, 0), 1.0)
        si = jnp.where(r >= d, pltpu.roll(pi, d, 0), 0.0)
        pr, pi = _cmul(pr, pi, sr, si)
        d *= 2
    pre_ref[...] = pr
    pim_ref[...] = pi


def _s5_prep_call(lam_re, lam_im, log_dt, b_re_t, b_im_t):
    sds = jax.ShapeDtypeStruct
    return pl.pallas_call(
        _s5_prep_kernel,
        out_shape=[sds((S5_TILE, S5_N), F32), sds((S5_TILE, S5_N), F32),
                   sds((S5_CH, S5_N), F32), sds((S5_CH, S5_N), F32)],
        compiler_params=pltpu.CompilerParams(vmem_limit_bytes=VMEM_LIMIT), name="s5_prep")(
            lam_re, lam_im, log_dt, b_re_t, b_im_t)


def _s5_kernel(u_ref, wbu_ref, pre_ref, pim_ref, wcr_ref, wci_ref, d_ref, wg_ref, bg_ref, h0r_ref, h0i_ref,
               o_ref, hr_ref, hi_ref, cr_sc, ci_sc, *, tt):
    i = pl.program_id(1)

    @pl.when(i == 0)
    def _():
        cr_sc[...] = h0r_ref[...]
        ci_sc[...] = h0i_ref[...]

    u = u_ref[...]
    bu = _dot(u, wbu_ref[...])
    xr, xi = bu[:, 0:S5_N], bu[:, S5_N:2 * S5_N]
    d = 1
    while d < tt:
        ar, ai = pre_ref[d - 1:d, :], pim_ref[d - 1:d, :]
        sr, si = _shift_rows(xr, d), _shift_rows(xi, d)
        mr, mi = _cmul(ar, ai, sr, si)
        xr, xi = xr + mr, xi + mi
        d *= 2
    cr, ci = cr_sc[...], ci_sc[...]
    mr, mi = _cmul(pre_ref[...], pim_ref[...], cr, ci)
    xr, xi = xr + mr, xi + mi
    cr_sc[...] = xr[tt - 1:tt, :]
    ci_sc[...] = xi[tt - 1:tt, :]
    hr_ref[...] = xr[tt - 1:tt, :]
    hi_ref[...] = xi[tt - 1:tt, :]
    y = _dot(xr, wcr_ref[...]) - _dot(xi, wci_ref[...]) + d_ref[...] * u
    z = _gelu_tanh(y)
    o_ref[...] = (z * _sigmoid(_dot(z, wg_ref[...]) + bg_ref[...])).astype(o_ref.dtype)


def _s5_call(u, wbu, pre, pim, wcr, wci, d_skip, wg, bg, h0r, h0i):
    b, t, _ = u.shape
    tt = _row_tile(t, S5_TILE)
    full = lambda a: pl.BlockSpec(a.shape, lambda bi, i: (0,) * a.ndim)
    pspec = pl.BlockSpec((tt, S5_N), lambda bi, i: (0, 0))
    hspec = pl.BlockSpec((None, 1, S5_N), lambda bi, i: (bi, 0, 0))
    row = pl.BlockSpec((None, tt, S5_W), lambda bi, i: (bi, i, 0))
    return pl.pallas_call(
        functools.partial(_s5_kernel, tt=tt), grid=(b, t // tt),
        in_specs=[row, full(wbu), pspec, pspec, full(wcr), full(wci), full(d_skip), full(wg), full(bg), hspec, hspec],
        out_specs=[row, hspec, hspec],
        out_shape=[jax.ShapeDtypeStruct((b, t, S5_W), BF16), jax.ShapeDtypeStruct((b, 1, S5_N), F32),
                   jax.ShapeDtypeStruct((b, 1, S5_N), F32)],
        scratch_shapes=[pltpu.VMEM((1, S5_N), F32), pltpu.VMEM((1, S5_N), F32)],
        compiler_params=_cparams(("parallel", "arbitrary")), name="s5")(
            u, wbu, pre, pim, wcr, wci, d_skip, wg, bg, h0r, h0i)


def _mix_ln_kernel(gla_ref, s5_ref, fox_ref, x_ref, w_ref, g_ref, b_ref, o_ref):
    mixed = jnp.dot(gla_ref[...], w_ref[0:GLA_W, :], preferred_element_type=F32)
    mixed = mixed + jnp.dot(s5_ref[...], w_ref[GLA_W:GLA_W + S5_W, :], preferred_element_type=F32)
    mixed = mixed + jnp.dot(fox_ref[...], w_ref[GLA_W + S5_W:, :], preferred_element_type=F32)
    o_ref[...] = _layer_norm(DN_ALPHA * x_ref[...] + mixed, g_ref[...], b_ref[...])


def _mix_ln_call(o_gla, o_s5, o_fox, x2, w, g, b):
    n = x2.shape[0]
    tm = _row_tile(n, 512)
    row = lambda w_: pl.BlockSpec((tm, w_), lambda i: (i, 0))
    full = lambda a: pl.BlockSpec(a.shape, lambda i: (0,) * a.ndim)
    return pl.pallas_call(
        _mix_ln_kernel, grid=(n // tm,),
        in_specs=[row(GLA_W), row(S5_W), row(FOX_W), row(D_MODEL), full(w), full(g), full(b)],
        out_specs=row(D_MODEL), out_shape=jax.ShapeDtypeStruct((n, D_MODEL), F32),
        compiler_params=_cparams(("parallel",)), name="mix_ln")(o_gla, o_s5, o_fox, x2, w, g, b)


def _mem_kv_kernel(m_ref, wk_ref, wv_ref, k_ref, v_ref):
    mb = m_ref[...].astype(BF16)
    k_ref[...] = jnp.dot(mb, wk_ref[...], preferred_element_type=F32)
    v_ref[...] = jnp.dot(mb, wv_ref[...], preferred_element_type=F32)


def _mem_kv_call(mem2, wk, wv):
    n = mem2.shape[0]
    tm = _row_tile(n, 256)
    row = pl.BlockSpec((tm, D_MODEL), lambda i: (i, 0))
    orow = pl.BlockSpec((tm, MEM_W), lambda i: (i, 0))
    full = lambda a: pl.BlockSpec(a.shape, lambda i: (0,) * a.ndim)
    sds = jax.ShapeDtypeStruct((n, MEM_W), F32)
    return pl.pallas_call(
        _mem_kv_kernel, grid=(n // tm,), in_specs=[row, full(wk), full(wv)], out_specs=[orow, orow],
        out_shape=[sds, sds], compiler_params=_cparams(("parallel",)), name="mem_kv")(mem2, wk, wv)


def _mem_attn_kernel(x_ref, wq_ref, mk_ref, mv_ref, wo_ref, g_ref, b_ref, o_ref):
    x = x_ref[...]
    q = jnp.dot(x.astype(BF16), wq_ref[...], preferred_element_type=F32).astype(BF16)
    mk = mk_ref[...].astype(BF16)
    mv = mv_ref[...].astype(BF16)
    heads = []
    for h in range(MEM_HEADS):
        sl = slice(h * MEM_HD, (h + 1) * MEM_HD)
        s = _dot_nt(q[:, sl], mk[:, sl]) * (MEM_HD ** -0.5)
        m = jnp.max(s, axis=1, keepdims=True)
        e = jnp.exp(s - m)
        p = e / jnp.sum(e, axis=1, keepdims=True)
        heads.append(_dot(p, mv[:, sl]).astype(BF16))
    o = jnp.concatenate(heads, axis=1)
    att = jnp.dot(o, wo_ref[...], preferred_element_type=F32)
    o_ref[...] = _layer_norm(DN_ALPHA * x + att, g_ref[...], b_ref[...])


def _mem_attn_call(x3, wq, mk, mv, wo, g, b):
    bsz, t, _ = x3.shape
    tm = _row_tile(t, 512)
    row = pl.BlockSpec((None, tm, D_MODEL), lambda bi, i: (bi, i, 0))
    mem = pl.BlockSpec((None, MEM_TOKENS, MEM_W), lambda bi, i: (bi, 0, 0))
    full = lambda a: pl.BlockSpec(a.shape, lambda bi, i: (0,) * a.ndim)
    return pl.pallas_call(
        _mem_attn_kernel, grid=(bsz, t // tm),
        in_specs=[row, full(wq), mem, mem, full(wo), full(g), full(b)], out_specs=row,
        out_shape=jax.ShapeDtypeStruct((bsz, t, D_MODEL), F32),
        compiler_params=_cparams(("parallel", "parallel")), name="mem_attn")(x3, wq, mk, mv, wo, g, b)


def _shift_with_halo(u, halos, d, seg):
    rolled = pltpu.roll(u, d, 0)
    r8 = lax.broadcasted_iota(jnp.int32, halos[0].shape, 0)
    pieces = []
    for s, halo in enumerate(halos):
        lo = s * seg
        pieces.append(jnp.where(r8 < d, pltpu.roll(halo, d, 0), rolled[lo:lo + SUBLANES, :]))
        pieces.append(rolled[lo + SUBLANES:lo + seg, :])
    return jnp.concatenate(pieces, axis=0)


def _ffn_kernel(x_ref, xh_ref, prev_ref, wu_ref, cw_ref, cb_ref, wd_ref, g_ref, b_ref, o_ref, tail_ref,
                *, tm, seg, tiles_per_seq):
    x = x_ref[...]
    xb = x.astype(BF16)
    nseg = tm // seg
    c = FFN_CHUNK
    if tiles_per_seq:
        xhb = xh_ref[...].astype(BF16)
        first = (pl.program_id(0) % tiles_per_seq) == 0

    def conv_branch(c0):
        w = wu_ref[:, c0:c0 + c]
        u = jnp.dot(xb, w, preferred_element_type=F32)
        if tiles_per_seq:
            uh = jnp.dot(xhb, w, preferred_element_type=F32)
            halos = [jnp.where(first, prev_ref[0, :, c0:c0 + c], uh)]
        else:
            halos = [prev_ref[s, :, c0:c0 + c] for s in range(nseg)]
        for s in range(nseg):
            tail_ref[s, :, c0:c0 + c] = u[(s + 1) * seg - SUBLANES:(s + 1) * seg, :]
        y = cb_ref[:, c0:c0 + c] + cw_ref[0:1, c0:c0 + c] * _shift_with_halo(u, halos, 2, seg)
        y = y + cw_ref[1:2, c0:c0 + c] * _shift_with_halo(u, halos, 1, seg)
        return y + cw_ref[2:3, c0:c0 + c] * u

    acc = None
    for j in range(FFN_NCH):
        a = conv_branch(j * c)
        g = conv_branch(D_FF + j * c)
        hidden = (_gelu_tanh(a) * g).astype(BF16)
        part = jnp.dot(hidden, wd_ref[j * c:(j + 1) * c, :], preferred_element_type=F32)
        acc = part if acc is None else acc + part
    o_ref[...] = _layer_norm(DN_ALPHA * x + acc, g_ref[...], b_ref[...])


def _ffn_call(x3, w_up, conv_w, conv_b, w_down, prev8, g, b):
    bsz, t, _ = x3.shape
    n = bsz * t
    tm = min(FFN_TM, n)
    assert n % tm == 0 and (t % tm == 0 or tm % t == 0)
    seg = min(t, tm)
    nseg = tm // seg
    tiles_per_seq = t // tm
    x2 = x3.reshape(n, D_MODEL)
    row = pl.BlockSpec((tm, D_MODEL), lambda i: (i, 0))
    halo = pl.BlockSpec((SUBLANES, D_MODEL), lambda i: (jnp.maximum(i * (tm // SUBLANES) - 1, 0), 0))
    if tiles_per_seq:
        prev = pl.BlockSpec((1, SUBLANES, 2 * D_FF), lambda i: (i // tiles_per_seq, 0, 0))
    else:
        prev = pl.BlockSpec((nseg, SUBLANES, 2 * D_FF), lambda i: (i, 0, 0))
    resident = lambda a: pl.BlockSpec(a.shape, lambda i: (0,) * a.ndim, pipeline_mode=pl.Buffered(1))
    y, tail = pl.pallas_call(
        functools.partial(_ffn_kernel, tm=tm, seg=seg, tiles_per_seq=tiles_per_seq), grid=(n // tm,),
        in_specs=[row, halo, prev, resident(w_up), resident(conv_w), resident(conv_b), resident(w_down),
                  resident(g), resident(b)],
        out_specs=[row, pl.BlockSpec((None, nseg, SUBLANES, 2 * D_FF), lambda i: (i, 0, 0, 0))],
        out_shape=[jax.ShapeDtypeStruct((n, D_MODEL), F32),
                   jax.ShapeDtypeStruct((n // tm, nseg, SUBLANES, 2 * D_FF), F32)],
        compiler_params=_cparams(("parallel",)), name="conv_ffn")(
            x2, x2, prev8, w_up, conv_w, conv_b, w_down, g, b)
    if tiles_per_seq:
        tail = tail.reshape(bsz, tiles_per_seq, SUBLANES, 2 * D_FF)[:, -1]
    else:
        tail = tail.reshape(bsz, SUBLANES, 2 * D_FF)
    return y.reshape(bsz, t, D_MODEL), tail


def _block_diag(blocks):
    g, r, c = blocks.shape
    eye = jnp.eye(g, dtype=blocks.dtype)
    return (blocks[:, :, None, :] * eye[:, None, :, None]).reshape(g * r, g * c)


def _prep_layer(l, p):
    w = p['w_in'][l]
    offs = [0, 128, 256, 512, 768, 784, 1040, 1552, 2064, 2576, 2584]
    gq, gk, gv, go, ga, su, fq, fk, fv, ff = [w[:, offs[n]:offs[n + 1]] for n in range(10)]
    pad = jnp.zeros((D_MODEL, LANES - FOX_HEADS - GLA_RANK), F32)
    w_r = jnp.concatenate([gq, gk, gv, go, su, fq, fk, fv, ff, ga, pad], axis=1).astype(BF16)
    pad_heads = lambda m: jnp.pad(m.reshape(D_MODEL, FOX_HEADS, FOX_HD), ((0, 0), (0, 0), (0, LANES - FOX_HD))
                                  ).reshape(D_MODEL, FOX_AUG_W)
    wqt_pad = pad_heads(fq).T.astype(BF16)
    wk_pad = pad_heads(fk).astype(BF16)
    wvt = fv.T.astype(BF16)
    wa_p = jnp.zeros((LANES, GLA_QK), F32).at[SM_GA:SM_GA + GLA_RANK].set(p['gla_w_a2'][l]).astype(BF16)
    row = lambda a: a.reshape(1, -1)
    flat = lambda a: a.reshape(1, S5_N)
    b_t = lambda a: jnp.transpose(a, (2, 0, 1)).reshape(S5_CH, S5_N)
    pre, pim, bb_re, bb_im = _s5_prep_call(flat(p['s5_lam_re'][l]), flat(p['s5_lam_im'][l]),
                                           flat(p['s5_log_dt'][l]), b_t(p['s5_b_re'][l]), b_t(p['s5_b_im'][l]))
    bd_in = lambda bb: _block_diag(jnp.transpose(bb.reshape(S5_CH, S5_GROUPS, S5_STATE), (1, 0, 2)))
    wbu = jnp.concatenate([bd_in(bb_re), bd_in(bb_im)], axis=1).astype(BF16)
    bd_out = lambda cm: _block_diag(jnp.transpose(cm, (0, 2, 1))).astype(BF16)
    return dict(
        w_r=w_r, wqt_pad=wqt_pad, wk_pad=wk_pad, wvt=wvt,
        b_f=row(p['fox_b_f'][l]), wa_p=wa_p, ba=row(p['gla_b_a'][l]),
        ng4=jnp.tile(p['gla_norm_g'][l], GLA_HEADS).reshape(1, GLA_W),
        wbu=wbu, pre=pre, pim=pim, wcr=bd_out(p['s5_c_re'][l]), wci=bd_out(p['s5_c_im'][l]),
        d_skip=row(p['s5_d'][l]), wg=p['s5_w_glu'][l].astype(BF16), bg=row(p['s5_b_glu'][l]),
        w_mix=p['w_mix_out'][l].astype(BF16), ln1_g=row(p['ln1_g'][l]), ln1_b=row(p['ln1_b'][l]),
        wq=p['mem_w_q'][l].astype(BF16), wk=p['mem_w_k'][l].astype(BF16), wv=p['mem_w_v'][l].astype(BF16),
        wo=p['mem_w_o'][l].astype(BF16), ln2_g=row(p['ln2_g'][l]), ln2_b=row(p['ln2_b'][l]),
        w_up=p['ffn_w_up'][l].astype(BF16), conv_w=p['ffn_conv_w'][l], conv_b=row(p['ffn_conv_b'][l]),
        w_down=p['ffn_w_down'][l].astype(BF16), ln3_g=row(p['ln3_g'][l]), ln3_b=row(p['ln3_b'][l]))


def _gla_state_in(s0):
    eye = jnp.eye(GLA_HEADS, dtype=s0.dtype)
    st = jnp.transpose(s0, (0, 1, 3, 2))
    full = st[:, :, :, None, :] * eye[None, :, None, :, None]
    return full.reshape(s0.shape[0], GLA_W, GLA_QK)


def _gla_state_out(st):
    s5 = st.reshape(st.shape[0], GLA_HEADS, GLA_DV, GLA_HEADS, GLA_DK)
    diag = jnp.stack([s5[:, h, :, h, :] for h in range(GLA_HEADS)], axis=1)
    return jnp.transpose(diag, (0, 1, 3, 2))


def _layer(x3, mem_k, mem_v, gla_s0, s5_h0_re, s5_h0_im, past_k, past_v, past_logf, conv_prev, lp):
    bsz, t, _ = x3.shape
    n = bsz * t
    x2 = x3.reshape(n, D_MODEL)
    gqk, gv, go, su, fk, fv, sm, logf3, *fox_ops = _in_proj_call(x3, lp, fox_t=past_k is None)

    o_gla, st_t = _gla_call(gqk, gv, go, sm, lp['wa_p'], lp['ba'], lp['ng4'], _gla_state_in(gla_s0))
    o_s5, h_re, h_im = _s5_call(su, lp['wbu'], lp['pre'], lp['pim'], lp['wcr'], lp['wci'], lp['d_skip'],
                                lp['wg'], lp['bg'], s5_h0_re.reshape(bsz, 1, S5_N), s5_h0_im.reshape(bsz, 1, S5_N))

    if past_k is None:
        qt_pad, k_pad, vt = fox_ops
        k_aug, qt_aug = _fox_aug_call(logf3, k_pad, qt_pad)
        o_fox = _flash_t_call(qt_aug, k_aug, vt)
    else:
        past = past_k.shape[1]
        k_all = jnp.concatenate([past_k.reshape(bsz, past, FOX_W), fk], axis=1)
        v_all = jnp.concatenate([past_v.reshape(bsz, past, FOX_W), fv], axis=1)
        cum_f = _cumsum_call(jnp.concatenate([past_logf, logf3], axis=1))
        o_fox = _flash_call(fox_ops[0], k_all, v_all, cum_f[:, past:], jnp.swapaxes(cum_f, 1, 2), past)

    x1 = _mix_ln_call(o_gla.reshape(n, GLA_W), o_s5.reshape(n, S5_W), o_fox.reshape(n, FOX_W), x2,
                      lp['w_mix'], lp['ln1_g'], lp['ln1_b'])
    x2b = _mem_attn_call(x1.reshape(bsz, t, D_MODEL), lp['wq'], mem_k, mem_v, lp['wo'], lp['ln2_g'], lp['ln2_b'])
    prev8 = jnp.concatenate([jnp.zeros((bsz, SUBLANES - (CONV_W - 1), 2 * D_FF), F32), conv_prev], axis=1)
    y3, tail = _ffn_call(x2b, lp['w_up'], lp['conv_w'], lp['conv_b'], lp['w_down'], prev8,
                         lp['ln3_g'], lp['ln3_b'])
    conv_new = tail[:, SUBLANES - (CONV_W - 1):, :]
    states = (_gla_state_out(st_t), h_re.reshape(bsz, S5_GROUPS, S5_STATE), h_im.reshape(bsz, S5_GROUPS, S5_STATE),
              fk.reshape(bsz, t, FOX_HEADS, FOX_HD), fv.reshape(bsz, t, FOX_HEADS, FOX_HD), logf3, conv_new)
    return y3, states


def kernel(x_prompt, x_sample, mem_prompt, state_gla, state_s5_re, state_s5_im, cache_fox_k, cache_fox_v,
           cache_fox_logf, cache_mem_k, cache_mem_v, state_ffn_conv, ln_in_g, ln_in_b, w_in, gla_w_a2, gla_b_a,
           gla_norm_g, s5_lam_re, s5_lam_im, s5_log_dt, s5_b_re, s5_b_im, s5_c_re, s5_c_im, s5_d, s5_w_glu,
           s5_b_glu, fox_b_f, w_mix_out, ln1_g, ln1_b, mem_w_q, mem_w_k, mem_w_v, mem_w_o, ln2_g, ln2_b,
           ffn_w_up, ffn_conv_w, ffn_conv_b, ffn_w_down, ln3_g, ln3_b):
    params = dict(w_in=w_in, gla_w_a2=gla_w_a2, gla_b_a=gla_b_a, gla_norm_g=gla_norm_g, s5_lam_re=s5_lam_re,
                  s5_lam_im=s5_lam_im, s5_log_dt=s5_log_dt, s5_b_re=s5_b_re, s5_b_im=s5_b_im, s5_c_re=s5_c_re,
                  s5_c_im=s5_c_im, s5_d=s5_d, s5_w_glu=s5_w_glu, s5_b_glu=s5_b_glu, fox_b_f=fox_b_f,
                  w_mix_out=w_mix_out, ln1_g=ln1_g, ln1_b=ln1_b, mem_w_q=mem_w_q, mem_w_k=mem_w_k,
                  mem_w_v=mem_w_v, mem_w_o=mem_w_o, ln2_g=ln2_g, ln2_b=ln2_b, ffn_w_up=ffn_w_up,
                  ffn_conv_w=ffn_conv_w, ffn_conv_b=ffn_conv_b, ffn_w_down=ffn_w_down, ln3_g=ln3_g, ln3_b=ln3_b)
    bp, tp, _ = x_prompt.shape
    bs, ts, _ = x_sample.shape
    g_in, b_in = ln_in_g.reshape(1, D_MODEL), ln_in_b.reshape(1, D_MODEL)
    hp = _ln_call(x_prompt.reshape(bp * tp, D_MODEL), g_in, b_in).reshape(bp, tp, D_MODEL)
    hs = _ln_call(x_sample.reshape(bs * ts, D_MODEL), g_in, b_in).reshape(bs, ts, D_MODEL)
    mem2 = mem_prompt.reshape(bp * MEM_TOKENS, D_MODEL)
    zero_gla = jnp.zeros((bp, GLA_HEADS, GLA_DK, GLA_DV), F32)
    zero_s5 = jnp.zeros((bp, S5_GROUPS, S5_STATE), F32)
    zero_conv = jnp.zeros((bp, CONV_W - 1, 2 * D_FF), F32)
    prompt_states, sample_states = [], []
    for l in range(DEPTH):
        lp = _prep_layer(l, params)
        mk2, mv2 = _mem_kv_call(mem2, lp['wk'], lp['wv'])
        mk_p = mk2.reshape(bp, MEM_TOKENS, MEM_W)
        mv_p = mv2.reshape(bp, MEM_TOKENS, MEM_W)
        hp, st_p = _layer(hp, mk_p, mv_p, zero_gla, zero_s5, zero_s5, None, None, None, zero_conv, lp)
        prompt_states.append(st_p + (mk_p.reshape(bp, MEM_TOKENS, MEM_HEADS, MEM_HD),
                                     mv_p.reshape(bp, MEM_TOKENS, MEM_HEADS, MEM_HD)))
        hs, st_s = _layer(hs, cache_mem_k[l].reshape(bs, MEM_TOKENS, MEM_W),
                          cache_mem_v[l].reshape(bs, MEM_TOKENS, MEM_W), state_gla[l], state_s5_re[l],
                          state_s5_im[l], cache_fox_k[l], cache_fox_v[l], cache_fox_logf[l], state_ffn_conv[l], lp)
        sample_states.append(st_s)
    p_out = [jnp.stack(z) for z in zip(*prompt_states)]
    s_out = [jnp.stack(z) for z in zip(*sample_states)]
    return (hp, hs, *p_out, *s_out)
```

```python
import functools
import math

import jax
import jax.numpy as jnp
from jax import lax
from jax.experimental import pallas as pl
from jax.experimental.pallas import tpu as pltpu

F32 = jnp.float32
BF16 = jnp.bfloat16

D_MODEL = 1024
DEPTH = 2
CHUNK = 64
GLA_HEADS = 4
GLA_DK = 32
GLA_DV = 64
GLA_RANK = 16
GLA_TAU = 16.0
S5_GROUPS = 16
S5_CH = 16
S5_STATE = 64
FOX_HEADS = 8
FOX_HD = 64
MEM_TOKENS = 256
MEM_HEADS = 4
MEM_HD = 256
D_FF = 2816
CONV_W = 3
LN_EPS = 1e-5
DN_ALPHA = (2 * DEPTH) ** 0.25

GLA_QK = GLA_HEADS * GLA_DK
GLA_W = GLA_HEADS * GLA_DV
S5_W = S5_GROUPS * S5_CH
S5_N = S5_GROUPS * S5_STATE
FOX_W = FOX_HEADS * FOX_HD
MEM_W = MEM_HEADS * MEM_HD

LANES = 128
SUBLANES = 8
VMEM_LIMIT = 56 * 1024 * 1024

C_GQK = 0
C_GV = 256
C_GO = 512
C_SU = 768
C_FQ = 1024
C_FK = 1536
C_FV = 2048
C_SM = 2560
N_INR = 2688
SM_FF = 0
SM_GA = FOX_HEADS

FFN_CHUNK = 256
FFN_NCH = D_FF // FFN_CHUNK
FFN_TM = 512
S5_TILE = 256
FOX_SCALE = FOX_HD ** -0.5
LOG2E = math.log2(math.e)
FOX_AUG_W = FOX_HEADS * LANES
FOX_NPIECE = 3
FOX_TQ = 512
FOX_TK = 512


def _cparams(sem):
    return pltpu.CompilerParams(dimension_semantics=sem, vmem_limit_bytes=VMEM_LIMIT)


def _dot(a, b):
    return jnp.dot(a.astype(BF16), b.astype(BF16), preferred_element_type=F32)


def _dot_nt(a, b):
    return lax.dot_general(a.astype(BF16), b.astype(BF16), (((1,), (1,)), ((), ())),
                           preferred_element_type=F32)


def _dot_tn(a, b):
    return lax.dot_general(a.astype(BF16), b.astype(BF16), (((0,), (0,)), ((), ())),
                           preferred_element_type=F32)


def _split3(x):
    hi = x.astype(BF16)
    r1 = x - hi.astype(F32)
    mid = r1.astype(BF16)
    lo = (r1 - mid.astype(F32)).astype(BF16)
    return hi, mid, lo


def _tri(n):
    r = lax.broadcasted_iota(jnp.int32, (n, n), 0)
    c = lax.broadcasted_iota(jnp.int32, (n, n), 1)
    return (c <= r).astype(BF16)


def _cumsum_rows(x, tri):
    hi, mid, lo = _split3(x)
    d = lambda p: jnp.dot(tri, p, preferred_element_type=F32)
    return d(hi) + (d(mid) + d(lo))


def _layer_norm(x, g, b):
    mu = jnp.mean(x, axis=-1, keepdims=True)
    xc = x - mu
    var = jnp.mean(xc * xc, axis=-1, keepdims=True)
    return xc * lax.rsqrt(var + LN_EPS) * g + b


def _log_sigmoid(x):
    return jnp.minimum(x, 0.0) - jnp.log1p(jnp.exp(-jnp.abs(x)))


def _sigmoid(x):
    return 1.0 / (1.0 + jnp.exp(-x))


def _gelu_tanh(x):
    c = math.sqrt(2.0 / math.pi)
    return 0.5 * x * (1.0 + jnp.tanh(c * (x + 0.044715 * (x * x * x))))


def _row_tile(n, want):
    t = min(n, want)
    assert n % t == 0
    return t


def _ln_kernel(x_ref, g_ref, b_ref, o_ref):
    o_ref[...] = _layer_norm(x_ref[...], g_ref[...], b_ref[...])


def _ln_call(x2, g, b):
    n = x2.shape[0]
    tm = _row_tile(n, 512)
    row = pl.BlockSpec((tm, D_MODEL), lambda i: (i, 0))
    vec = pl.BlockSpec((1, D_MODEL), lambda i: (0, 0))
    return pl.pallas_call(
        _ln_kernel, grid=(n // tm,), in_specs=[row, vec, vec], out_specs=row,
        out_shape=jax.ShapeDtypeStruct((n, D_MODEL), F32),
        compiler_params=_cparams(("parallel",)), name="ln_in")(x2, g, b)


def _in_proj_kernel(x_ref, w_ref, bf_ref, gqk_ref, gv_ref, go_ref, su_ref, fk_ref, fv_ref, sm_ref, logf_ref,
                    fq_ref):
    xb = x_ref[...].astype(BF16)
    proj = lambda lo, hi: jnp.dot(xb, w_ref[:, lo:hi], preferred_element_type=F32)
    gqk_ref[...] = proj(C_GQK, C_GV)
    gv_ref[...] = proj(C_GV, C_GO)
    go_ref[...] = proj(C_GO, C_SU)
    su_ref[...] = proj(C_SU, C_FQ)
    fk_ref[...] = proj(C_FK, C_FV)
    fv_ref[...] = proj(C_FV, C_SM)
    sm = proj(C_SM, N_INR)
    sm_ref[...] = sm
    logf_ref[...] = _log_sigmoid(sm[:, SM_FF:SM_FF + FOX_HEADS] + bf_ref[...])
    fq_ref[...] = (proj(C_FQ, C_FK) * FOX_SCALE).astype(BF16)


def _in_proj_t_kernel(x_ref, w_ref, bf_ref, wqt_ref, wkp_ref, wvt_ref, gqk_ref, gv_ref, go_ref, su_ref, fk_ref,
                      fv_ref, sm_ref, logf_ref, qt_ref, kp_ref, vt_ref):
    xb = x_ref[...].astype(BF16)
    proj = lambda lo, hi: jnp.dot(xb, w_ref[:, lo:hi], preferred_element_type=F32)
    gqk_ref[...] = proj(C_GQK, C_GV)
    gv_ref[...] = proj(C_GV, C_GO)
    go_ref[...] = proj(C_GO, C_SU)
    su_ref[...] = proj(C_SU, C_FQ)
    fk_ref[...] = proj(C_FK, C_FV)
    fv_ref[...] = proj(C_FV, C_SM)
    sm = proj(C_SM, N_INR)
    sm_ref[...] = sm
    logf_ref[...] = _log_sigmoid(sm[:, SM_FF:SM_FF + FOX_HEADS] + bf_ref[...])
    qt_ref[...] = (_dot_nt(wqt_ref[...], xb) * (FOX_SCALE * LOG2E)).astype(BF16)
    kp_ref[...] = jnp.dot(xb, wkp_ref[...], preferred_element_type=F32).astype(BF16)
    vt_ref[...] = _dot_nt(wvt_ref[...], xb).astype(BF16)


def _in_proj_call(x3, lp, fox_t):
    bsz, t, _ = x3.shape
    tm = _row_tile(t, 512)
    row = lambda w: pl.BlockSpec((None, tm, w), lambda bi, i: (bi, i, 0))
    col = lambda r: pl.BlockSpec((None, r, tm), lambda bi, i: (bi, 0, i))
    full = lambda a: pl.BlockSpec(a.shape, lambda bi, i: (0,) * a.ndim)
    sds = lambda w, dt: jax.ShapeDtypeStruct((bsz, t, w), dt)
    in_arrays = [x3, lp['w_r'], lp['b_f']]
    out_specs = [row(256), row(256), row(256), row(256), row(FOX_W), row(FOX_W), row(LANES), row(FOX_HEADS)]
    out_shape = [sds(256, F32), sds(256, F32), sds(256, F32), sds(256, F32), sds(FOX_W, F32), sds(FOX_W, F32),
                 sds(LANES, F32), sds(FOX_HEADS, F32)]
    if fox_t:
        body = _in_proj_t_kernel
        in_arrays += [lp['wqt_pad'], lp['wk_pad'], lp['wvt']]
        out_specs += [col(FOX_AUG_W), row(FOX_AUG_W), col(FOX_W)]
        out_shape += [jax.ShapeDtypeStruct((bsz, FOX_AUG_W, t), BF16), sds(FOX_AUG_W, BF16),
                      jax.ShapeDtypeStruct((bsz, FOX_W, t), BF16)]
    else:
        body = _in_proj_kernel
        out_specs += [row(FOX_W)]
        out_shape += [sds(FOX_W, BF16)]
    in_specs = [row(D_MODEL)] + [full(a) for a in in_arrays[1:]]
    return pl.pallas_call(
        body, grid=(bsz, t // tm), in_specs=in_specs, out_specs=out_specs, out_shape=out_shape,
        compiler_params=_cparams(("parallel", "parallel")), name="in_proj")(*in_arrays)


def _cumsum_kernel(x_ref, o_ref, *, t, blk):
    tri = _tri(blk)
    carry = jnp.zeros((1, FOX_HEADS), F32)
    for c in range(t // blk):
        cs = _cumsum_rows(x_ref[c * blk:(c + 1) * blk, :], tri) + carry
        o_ref[c * blk:(c + 1) * blk, :] = cs
        carry = cs[blk - 1:blk, :]


def _cumsum_call(x3):
    b, t, h = x3.shape
    blk = max(c for c in range(SUBLANES, 1024 + 1, SUBLANES) if t % c == 0)
    spec = pl.BlockSpec((None, t, h), lambda i: (i, 0, 0))
    return pl.pallas_call(
        functools.partial(_cumsum_kernel, t=t, blk=blk), grid=(b,), in_specs=[spec], out_specs=spec,
        out_shape=jax.ShapeDtypeStruct((b, t, h), F32),
        compiler_params=_cparams(("parallel",)), name="fox_cumsum")(x3)


def _fox_aug_kernel(logf_ref, kp_ref, qt_ref, ko_ref, qo_ref, carry_sc, *, tt):
    @pl.when(pl.program_id(1) == 0)
    def _():
        carry_sc[...] = jnp.zeros(carry_sc.shape, F32)

    cum = _cumsum_rows(logf_ref[...], _tri(tt)) + carry_sc[...]
    carry_sc[...] = cum[tt - 1:tt, :]
    pieces = jnp.concatenate([p.astype(F32) for p in _split3(cum * LOG2E)], axis=1)
    npc = FOX_NPIECE * FOX_HEADS

    lane = lax.broadcasted_iota(jnp.int32, (npc, FOX_AUG_W), 1)
    src_k = lax.broadcasted_iota(jnp.int32, (npc, FOX_AUG_W), 0)
    place_k = lane == (src_k % FOX_HEADS) * LANES + FOX_HD + FOX_NPIECE + src_k // FOX_HEADS
    aug_k = -jnp.dot(pieces, place_k.astype(F32), preferred_element_type=F32)
    off_k = lax.broadcasted_iota(jnp.int32, (1, FOX_AUG_W), 1) & (LANES - 1)
    ones_k = jnp.logical_and(off_k >= FOX_HD, off_k < FOX_HD + FOX_NPIECE).astype(F32)
    ko_ref[...] = (kp_ref[...].astype(F32) + aug_k + ones_k).astype(BF16)

    row = lax.broadcasted_iota(jnp.int32, (FOX_AUG_W, npc), 0)
    src_q = lax.broadcasted_iota(jnp.int32, (FOX_AUG_W, npc), 1)
    place_q = row == (src_q % FOX_HEADS) * LANES + FOX_HD + src_q // FOX_HEADS
    aug_q = lax.dot_general(place_q.astype(F32), pieces, (((1,), (1,)), ((), ())), preferred_element_type=F32)
    off_q = lax.broadcasted_iota(jnp.int32, (FOX_AUG_W, 1), 0) & (LANES - 1)
    ones_q = jnp.logical_and(off_q >= FOX_HD + FOX_NPIECE, off_q < FOX_HD + 2 * FOX_NPIECE).astype(F32)
    qo_ref[...] = (qt_ref[...].astype(F32) + aug_q + ones_q).astype(BF16)


def _fox_aug_call(logf3, k_pad, qt_pad):
    bsz, t, _ = logf3.shape
    tt = _row_tile(t, 512)
    kspec = pl.BlockSpec((None, tt, FOX_AUG_W), lambda bi, i: (bi, i, 0))
    qspec = pl.BlockSpec((None, FOX_AUG_W, tt), lambda bi, i: (bi, 0, i))
    return pl.pallas_call(
        functools.partial(_fox_aug_kernel, tt=tt), grid=(bsz, t // tt),
        in_specs=[pl.BlockSpec((None, tt, FOX_HEADS), lambda bi, i: (bi, i, 0)), kspec, qspec],
        out_specs=[kspec, qspec],
        out_shape=[jax.ShapeDtypeStruct(k_pad.shape, BF16), jax.ShapeDtypeStruct(qt_pad.shape, BF16)],
        scratch_shapes=[pltpu.VMEM((1, FOX_HEADS), F32)],
        input_output_aliases={1: 0, 2: 1},
        compiler_params=_cparams(("parallel", "arbitrary")), name="fox_aug")(logf3, k_pad, qt_pad)


def _flash_t_kernel(qi_ref, kj_ref, last_ref, qt_ref, k_ref, vt_ref, o_ref, m_sc, l_sc, acc_sc, *, tq, tk):
    step = pl.program_id(1)
    i = qi_ref[step]
    j = kj_ref[step]

    @pl.when(j == 0)
    def _():
        m_sc[...] = jnp.full(m_sc.shape, -jnp.inf, F32)
        l_sc[...] = jnp.zeros(l_sc.shape, F32)
        acc_sc[...] = jnp.zeros(acc_sc.shape, F32)

    def compute(masked):
        if masked:
            kpos = j * tk + lax.broadcasted_iota(jnp.int32, (tk, tq), 0)
            qpos = i * tq + lax.broadcasted_iota(jnp.int32, (tk, tq), 1)
            visible = kpos <= qpos
        scores = lambda h: jnp.dot(k_ref[:, h * LANES:(h + 1) * LANES], qt_ref[h * LANES:(h + 1) * LANES, :],
                                   preferred_element_type=F32)
        s_next = scores(0)
        for h in range(FOX_HEADS):
            s_t = s_next
            if h + 1 < FOX_HEADS:
                s_next = scores(h + 1)
            if masked:
                s_t = jnp.where(visible, s_t, -jnp.inf)
            m_prev = m_sc[h:h + 1, :]
            m_new = jnp.maximum(m_prev, jnp.max(s_t, axis=0, keepdims=True))
            alpha = jnp.exp2(m_prev - m_new)
            p_t = jnp.exp2(s_t - m_new)
            l_sc[h:h + 1, :] = alpha * l_sc[h:h + 1, :] + jnp.sum(p_t, axis=0, keepdims=True)
            rows = slice(h * FOX_HD, (h + 1) * FOX_HD)
            acc_sc[rows, :] = acc_sc[rows, :] * alpha + jnp.dot(vt_ref[rows, :], p_t.astype(BF16),
                                                                preferred_element_type=F32)
            m_sc[h:h + 1, :] = m_new

    fully_visible = j * tk + (tk - 1) <= i * tq

    @pl.when(fully_visible)
    def _():
        compute(False)

    @pl.when(jnp.logical_not(fully_visible))
    def _():
        compute(True)

    @pl.when(last_ref[step] == 1)
    def _():
        for h in range(FOX_HEADS):
            rows = slice(h * FOX_HD, (h + 1) * FOX_HD)
            acc_sc[rows, :] = acc_sc[rows, :] / l_sc[h:h + 1, :]
        o_ref[...] = acc_sc[...].T.astype(o_ref.dtype)


def _flash_t_call(qt_aug, k_aug, vt):
    bsz, _, t = qt_aug.shape
    tq, tk = _row_tile(t, FOX_TQ), _row_tile(t, FOX_TK)
    pairs = [(i, j) for i in range(t // tq) for j in range((i * tq + tq - 1) // tk + 1)]
    qi = jnp.asarray([p[0] for p in pairs], jnp.int32)
    kj = jnp.asarray([p[1] for p in pairs], jnp.int32)
    last = jnp.asarray([int(n + 1 == len(pairs) or pairs[n + 1][0] != p[0]) for n, p in enumerate(pairs)],
                       jnp.int32)
    grid_spec = pltpu.PrefetchScalarGridSpec(
        num_scalar_prefetch=3, grid=(bsz, len(pairs)),
        in_specs=[pl.BlockSpec((None, FOX_AUG_W, tq), lambda bi, s, qi, kj, last: (bi, 0, qi[s])),
                  pl.BlockSpec((None, tk, FOX_AUG_W), lambda bi, s, qi, kj, last: (bi, kj[s], 0)),
                  pl.BlockSpec((None, FOX_W, tk), lambda bi, s, qi, kj, last: (bi, 0, kj[s]))],
        out_specs=pl.BlockSpec((None, tq, FOX_W), lambda bi, s, qi, kj, last: (bi, qi[s], 0)),
        scratch_shapes=[pltpu.VMEM((FOX_HEADS, tq), F32), pltpu.VMEM((FOX_HEADS, tq), F32),
                        pltpu.VMEM((FOX_W, tq), F32)])
    return pl.pallas_call(
        functools.partial(_flash_t_kernel, tq=tq, tk=tk), grid_spec=grid_spec,
        out_shape=jax.ShapeDtypeStruct((bsz, t, FOX_W), BF16),
        compiler_params=_cparams(("parallel", "arbitrary")), name="fox_flash_t")(qi, kj, last, qt_aug, k_aug, vt)


def _flash_kernel(q_ref, k_ref, v_ref, fq_ref, fk_ref, o_ref, m_sc, l_sc, acc_sc, *, tq, tk, past, nkv):
    i = pl.program_id(1)
    j = pl.program_id(2)
    q_lo = past + i * tq

    @pl.when(j == 0)
    def _():
        m_sc[...] = jnp.full(m_sc.shape, -jnp.inf, F32)
        l_sc[...] = jnp.zeros(l_sc.shape, F32)
        acc_sc[...] = jnp.zeros(acc_sc.shape, F32)

    def compute(masked):
        if masked:
            qpos = q_lo + lax.broadcasted_iota(jnp.int32, (tq, tk), 0)
            kpos = j * tk + lax.broadcasted_iota(jnp.int32, (tq, tk), 1)
            visible = kpos <= qpos
        for h in range(FOX_HEADS):
            sl = slice(h * FOX_HD, (h + 1) * FOX_HD)
            s = _dot_nt(q_ref[:, sl], k_ref[:, sl])
            s = s + fq_ref[:, h:h + 1] - fk_ref[h:h + 1, :]
            if masked:
                s = jnp.where(visible, s, -jnp.inf)
            m_prev = m_sc[h]
            m_new = jnp.maximum(m_prev, jnp.max(s, axis=1, keepdims=True))
            alpha = jnp.exp(m_prev - m_new)
            p = jnp.exp(s - m_new[:, 0:1])
            l_sc[h] = alpha * l_sc[h] + jnp.sum(p, axis=1, keepdims=True)
            acc_sc[:, sl] = acc_sc[:, sl] * alpha[:, 0:FOX_HD] + _dot(p, v_ref[:, sl])
            m_sc[h] = m_new

    k_hi = j * tk + (tk - 1)
    needed = j * tk <= q_lo + (tq - 1)
    fully_visible = k_hi <= q_lo

    @pl.when(jnp.logical_and(needed, fully_visible))
    def _():
        compute(False)

    @pl.when(jnp.logical_and(needed, jnp.logical_not(fully_visible)))
    def _():
        compute(True)

    @pl.when(j == nkv - 1)
    def _():
        for h in range(FOX_HEADS):
            sl = slice(h * FOX_HD, (h + 1) * FOX_HD)
            o_ref[:, sl] = (acc_sc[:, sl] / l_sc[h][:, 0:FOX_HD]).astype(o_ref.dtype)


def _flash_call(q, k, v, cf_q, cf_k_t, past):
    b, tq_all, _ = q.shape
    tk_all = k.shape[1]
    tq = _row_tile(tq_all, 512)
    tk = tk_all if tk_all % 512 else 512
    nq, nkv = tq_all // tq, tk_all // tk

    def kv_idx(bi, i, j):
        last = (past + i * tq + tq - 1) // tk
        return (bi, jnp.minimum(j, last), 0)

    def fk_idx(bi, i, j):
        last = (past + i * tq + tq - 1) // tk
        return (bi, 0, jnp.minimum(j, last))

    qspec = pl.BlockSpec((None, tq, FOX_W), lambda bi, i, j: (bi, i, 0))
    return pl.pallas_call(
        functools.partial(_flash_kernel, tq=tq, tk=tk, past=past, nkv=nkv),
        grid=(b, nq, nkv),
        in_specs=[qspec,
                  pl.BlockSpec((None, tk, FOX_W), kv_idx),
                  pl.BlockSpec((None, tk, FOX_W), kv_idx),
                  pl.BlockSpec((None, tq, FOX_HEADS), lambda bi, i, j: (bi, i, 0)),
                  pl.BlockSpec((None, FOX_HEADS, tk), fk_idx)],
        out_specs=qspec,
        out_shape=jax.ShapeDtypeStruct((b, tq_all, FOX_W), BF16),
        scratch_shapes=[pltpu.VMEM((FOX_HEADS, tq, LANES), F32),
                        pltpu.VMEM((FOX_HEADS, tq, LANES), F32),
                        pltpu.VMEM((tq, FOX_W), F32)],
        compiler_params=_cparams(("parallel", "parallel", "arbitrary")), name="fox_flash")(
            q, k, v, cf_q, cf_k_t)


def _gla_kernel(gqk_ref, gv_ref, go_ref, sm_ref, wa_ref, ba_ref, ng_ref, s0_ref, o_ref, sT_ref, st_sc, *, tb):
    i = pl.program_id(1)

    @pl.when(i == 0)
    def _():
        st_sc[...] = s0_ref[...]

    nchunk = tb // CHUNK
    r_i = lax.broadcasted_iota(jnp.int32, (tb, tb), 0)
    c_i = lax.broadcasted_iota(jnp.int32, (tb, tb), 1)
    causal = jnp.logical_and(r_i // CHUNK == c_i // CHUNK, c_i <= r_i)
    tri = causal.astype(BF16)
    lane_k = lax.broadcasted_iota(jnp.int32, (1, GLA_QK), 1) // GLA_DK
    lane_v = lax.broadcasted_iota(jnp.int32, (1, GLA_W), 1) // GLA_DV
    row_v = lax.broadcasted_iota(jnp.int32, (GLA_W, GLA_QK), 0) // GLA_DV
    col_k = lax.broadcasted_iota(jnp.int32, (GLA_W, GLA_QK), 1) // GLA_DK
    blockdiag = (row_v == col_k).astype(F32)
    rr = lax.broadcasted_iota(jnp.int32, (GLA_W, GLA_W), 0) // GLA_DV
    cc = lax.broadcasted_iota(jnp.int32, (GLA_W, GLA_W), 1) // GLA_DV
    avg = jnp.where(rr == cc, 1.0 / GLA_DV, 0.0).astype(BF16)

    q = gqk_ref[:, 0:GLA_QK] * (GLA_DK ** -0.5)
    k = gqk_ref[:, GLA_QK:2 * GLA_QK]
    v = gv_ref[...]
    z = _dot(sm_ref[...], wa_ref[...]) + ba_ref[...]
    log_a = _log_sigmoid(z) / GLA_TAU
    cum = _cumsum_rows(log_a, tri)
    ends = [cum[(c + 1) * CHUNK - 1:(c + 1) * CHUNK, :] for c in range(nchunk)]
    cum_end = jnp.concatenate([jnp.broadcast_to(e, (CHUNK, GLA_QK)) for e in ends], axis=0)
    q_dec = q * jnp.exp(cum)
    k_inv = k * jnp.exp(-cum)
    k_end = k * jnp.exp(cum_end - cum)
    o = jnp.zeros((tb, GLA_W), F32)
    for h in range(GLA_HEADS):
        qh = jnp.where(lane_k == h, q_dec, 0.0)
        sc = jnp.where(causal, _dot_nt(qh, k_inv), 0.0)
        o = o + jnp.where(lane_v == h, _dot(sc, v), 0.0)
    st = st_sc[...]
    inter = []
    for c in range(nchunk):
        rows = slice(c * CHUNK, (c + 1) * CHUNK)
        inter.append(_dot_nt(q_dec[rows, :], st))
        st = st * jnp.exp(ends[c]) + _dot_tn(v[rows, :], k_end[rows, :]) * blockdiag
    st_sc[...] = st
    sT_ref[...] = st
    o = o + jnp.concatenate(inter, axis=0)
    o2 = o * o
    hi = o2.astype(BF16)
    lo = (o2 - hi.astype(F32)).astype(BF16)
    ms = jnp.dot(hi, avg, preferred_element_type=F32) + jnp.dot(lo, avg, preferred_element_type=F32)
    o = o * lax.rsqrt(ms + LN_EPS) * ng_ref[...]
    g = go_ref[...]
    o_ref[...] = (o * (g * _sigmoid(g))).astype(o_ref.dtype)


def _gla_call(gqk, gv, go, sm, wa_p, ba, ng4, s0_t):
    b, t, _ = gqk.shape
    tb = _row_tile(t, 256)
    row = lambda w: pl.BlockSpec((None, tb, w), lambda bi, i: (bi, i, 0))
    full = lambda a: pl.BlockSpec(a.shape, lambda bi, i: (0,) * a.ndim)
    st_spec = pl.BlockSpec((None, GLA_W, GLA_QK), lambda bi, i: (bi, 0, 0))
    return pl.pallas_call(
        functools.partial(_gla_kernel, tb=tb), grid=(b, t // tb),
        in_specs=[row(2 * GLA_QK), row(GLA_W), row(GLA_W), row(LANES), full(wa_p), full(ba), full(ng4), st_spec],
        out_specs=[row(GLA_W), st_spec],
        out_shape=[jax.ShapeDtypeStruct((b, t, GLA_W), BF16), jax.ShapeDtypeStruct((b, GLA_W, GLA_QK), F32)],
        scratch_shapes=[pltpu.VMEM((GLA_W, GLA_QK), F32)],
        compiler_params=_cparams(("parallel", "arbitrary")), name="gla")(gqk, gv, go, sm, wa_p, ba, ng4, s0_t)


def _cmul(ar, ai, br, bi):
    return ar * br - ai * bi, ar * bi + ai * br


def _s5_prep_kernel(lre_ref, lim_ref, ldt_ref, bre_ref, bim_ref, pre_ref, pim_ref, bbre_ref, bbim_ref):
    lam_re, lam_im = lre_ref[...], lim_ref[...]
    dt = jnp.exp(ldt_ref[...])
    mag = jnp.exp(lam_re * dt)
    ab_re = mag * jnp.cos(lam_im * dt)
    ab_im = mag * jnp.sin(lam_im * dt)
    den = lam_re * lam_re + lam_im * lam_im
    f_re = ((ab_re - 1.0) * lam_re + ab_im * lam_im) / den
    f_im = (ab_im * lam_re - (ab_re - 1.0) * lam_im) / den
    b_re, b_im = bre_ref[...], bim_ref[...]
    bbre_ref[...] = f_re * b_re - f_im * b_im
    bbim_ref[...] = f_re * b_im + f_im * b_re
    pr = jnp.broadcast_to(ab_re, (S5_TILE, S5_N))
    pi = jnp.broadcast_to(ab_im, (S5_TILE, S5_N))
    r = lax.broadcasted_iota(jnp.int32, (S5_TILE, S5_N), 0)
    d = 1
    while d < S5_TILE:
        sr = jnp.where(r >= d, pltpu.roll(pr, d, 0), 1.0)
        si = jnp.where(r >= d, pltpu.roll(pi, d, 0), 0.0)
        pr, pi = _cmul(pr, pi, sr, si)
        d *= 2
    pre_ref[...] = pr
    pim_ref[...] = pi


def _s5_prep_call(lam_re, lam_im, log_dt, b_re_t, b_im_t):
    sds = jax.ShapeDtypeStruct
    return pl.pallas_call(
        _s5_prep_kernel,
        out_shape=[sds((S5_TILE, S5_N), F32), sds((S5_TILE, S5_N), F32),
                   sds((S5_CH, S5_N), F32), sds((S5_CH, S5_N), F32)],
        compiler_params=pltpu.CompilerParams(vmem_limit_bytes=VMEM_LIMIT), name="s5_prep")(
            lam_re, lam_im, log_dt, b_re_t, b_im_t)


def _s5_pitch(tt):
    steps = tt // SUBLANES
    assert steps % SUBLANES == 0
    return steps if (steps // SUBLANES) % 2 else steps + SUBLANES


def _s5_kernel(u_ref, wbu_ref, pre_ref, pim_ref, wcr_ref, wci_ref, d_ref, wg_ref, bg_ref, h0r_ref, h0i_ref,
               o_ref, hr_ref, hi_ref, cr_sc, ci_sc, xr_sc, xi_sc, *, tt):
    i = pl.program_id(1)
    steps = tt // SUBLANES

    @pl.when(i == 0)
    def _():
        cr_sc[...] = h0r_ref[...]
        ci_sc[...] = h0i_ref[...]

    u = u_ref[...]
    bu = _dot(u, wbu_ref[...])
    ntile = S5_N // LANES
    lanes = lambda j: slice(j * LANES, (j + 1) * LANES)
    pitch = _s5_pitch(tt)
    seg_rows = lambda k: slice(k * pitch, k * pitch + steps)
    for j in range(ntile):
        for k in range(SUBLANES):
            xr_sc[j, seg_rows(k), :] = bu[k * steps:(k + 1) * steps, j * LANES:(j + 1) * LANES]
            xi_sc[j, seg_rows(k), :] = bu[k * steps:(k + 1) * steps, S5_N + j * LANES:S5_N + (j + 1) * LANES]
    xr = [jnp.zeros((SUBLANES, LANES), F32) for _ in range(ntile)]
    xi = [jnp.zeros((SUBLANES, LANES), F32) for _ in range(ntile)]
    for t in range(steps):
        rows = pl.ds(t, SUBLANES, stride=pitch)
        for j in range(ntile):
            mr, mi = _cmul(pre_ref[0:1, lanes(j)], pim_ref[0:1, lanes(j)], xr[j], xi[j])
            xr[j], xi[j] = mr + xr_sc[j, rows, :], mi + xi_sc[j, rows, :]
            xr_sc[j, rows, :] = xr[j]
            xi_sc[j, rows, :] = xi[j]
    end_r, end_i = jnp.concatenate(xr, axis=1), jnp.concatenate(xi, axis=1)
    sr, si = pre_ref[steps - 1:steps, :], pim_ref[steps - 1:steps, :]
    c_r, c_i = cr_sc[...], ci_sc[...]
    ent_r, ent_i = [], []
    for k in range(SUBLANES):
        ent_r.append(c_r)
        ent_i.append(c_i)
        mr, mi = _cmul(sr, si, c_r, c_i)
        c_r, c_i = end_r[k:k + 1, :] + mr, end_i[k:k + 1, :] + mi
    cr_sc[...] = c_r
    ci_sc[...] = c_i
    hr_ref[...] = c_r
    hi_ref[...] = c_i
    ent_r = jnp.concatenate(ent_r, axis=0)
    ent_i = jnp.concatenate(ent_i, axis=0)
    for t in range(steps):
        rows = pl.ds(t, SUBLANES, stride=pitch)
        for j in range(ntile):
            mr, mi = _cmul(pre_ref[t:t + 1, lanes(j)], pim_ref[t:t + 1, lanes(j)],
                           ent_r[:, lanes(j)], ent_i[:, lanes(j)])
            xr_sc[j, rows, :] = xr_sc[j, rows, :] + mr
            xi_sc[j, rows, :] = xi_sc[j, rows, :] + mi
    gather = lambda sc: jnp.concatenate(
        [jnp.concatenate([sc[j, seg_rows(k), :] for k in range(SUBLANES)], axis=0) for j in range(ntile)], axis=1)
    x_re = gather(xr_sc)
    x_im = gather(xi_sc)
    y = _dot(x_re, wcr_ref[...]) - _dot(x_im, wci_ref[...]) + d_ref[...] * u
    z = _gelu_tanh(y)
    o_ref[...] = (z * _sigmoid(_dot(z, wg_ref[...]) + bg_ref[...])).astype(o_ref.dtype)


def _s5_call(u, wbu, pre, pim, wcr, wci, d_skip, wg, bg, h0r, h0i):
    b, t, _ = u.shape
    tt = _row_tile(t, S5_TILE)
    full = lambda a: pl.BlockSpec(a.shape, lambda bi, i: (0,) * a.ndim)
    pspec = pl.BlockSpec((tt, S5_N), lambda bi, i: (0, 0))
    hspec = pl.BlockSpec((None, 1, S5_N), lambda bi, i: (bi, 0, 0))
    row = pl.BlockSpec((None, tt, S5_W), lambda bi, i: (bi, i, 0))
    return pl.pallas_call(
        functools.partial(_s5_kernel, tt=tt), grid=(b, t // tt),
        in_specs=[row, full(wbu), pspec, pspec, full(wcr), full(wci), full(d_skip), full(wg), full(bg), hspec, hspec],
        out_specs=[row, hspec, hspec],
        out_shape=[jax.ShapeDtypeStruct((b, t, S5_W), BF16), jax.ShapeDtypeStruct((b, 1, S5_N), F32),
                   jax.ShapeDtypeStruct((b, 1, S5_N), F32)],
        scratch_shapes=[pltpu.VMEM((1, S5_N), F32), pltpu.VMEM((1, S5_N), F32),
                        pltpu.VMEM((S5_N // LANES, SUBLANES * _s5_pitch(tt), LANES), F32),
                        pltpu.VMEM((S5_N // LANES, SUBLANES * _s5_pitch(tt), LANES), F32)],
        compiler_params=_cparams(("parallel", "arbitrary")), name="s5")(
            u, wbu, pre, pim, wcr, wci, d_skip, wg, bg, h0r, h0i)


def _mix_ln_kernel(gla_ref, s5_ref, fox_ref, x_ref, w_ref, g_ref, b_ref, o_ref):
    mixed = jnp.dot(gla_ref[...], w_ref[0:GLA_W, :], preferred_element_type=F32)
    mixed = mixed + jnp.dot(s5_ref[...], w_ref[GLA_W:GLA_W + S5_W, :], preferred_element_type=F32)
    mixed = mixed + jnp.dot(fox_ref[...], w_ref[GLA_W + S5_W:, :], preferred_element_type=F32)
    o_ref[...] = _layer_norm(DN_ALPHA * x_ref[...] + mixed, g_ref[...], b_ref[...])


def _mix_ln_call(o_gla, o_s5, o_fox, x2, w, g, b):
    n = x2.shape[0]
    tm = _row_tile(n, 512)
    row = lambda w_: pl.BlockSpec((tm, w_), lambda i: (i, 0))
    full = lambda a: pl.BlockSpec(a.shape, lambda i: (0,) * a.ndim)
    return pl.pallas_call(
        _mix_ln_kernel, grid=(n // tm,),
        in_specs=[row(GLA_W), row(S5_W), row(FOX_W), row(D_MODEL), full(w), full(g), full(b)],
        out_specs=row(D_MODEL), out_shape=jax.ShapeDtypeStruct((n, D_MODEL), F32),
        compiler_params=_cparams(("parallel",)), name="mix_ln")(o_gla, o_s5, o_fox, x2, w, g, b)


def _mem_kv_kernel(m_ref, wk_ref, wv_ref, k_ref, v_ref):
    mb = m_ref[...].astype(BF16)
    k_ref[...] = jnp.dot(mb, wk_ref[...], preferred_element_type=F32)
    v_ref[...] = jnp.dot(mb, wv_ref[...], preferred_element_type=F32)


def _mem_kv_call(mem2, wk, wv):
    n = mem2.shape[0]
    tm = _row_tile(n, 256)
    row = pl.BlockSpec((tm, D_MODEL), lambda i: (i, 0))
    orow = pl.BlockSpec((tm, MEM_W), lambda i: (i, 0))
    full = lambda a: pl.BlockSpec(a.shape, lambda i: (0,) * a.ndim)
    sds = jax.ShapeDtypeStruct((n, MEM_W), F32)
    return pl.pallas_call(
        _mem_kv_kernel, grid=(n // tm,), in_specs=[row, full(wk), full(wv)], out_specs=[orow, orow],
        out_shape=[sds, sds], compiler_params=_cparams(("parallel",)), name="mem_kv")(mem2, wk, wv)


def _mem_attn_kernel(x_ref, wq_ref, mk_ref, mv_ref, wo_ref, g_ref, b_ref, o_ref):
    x = x_ref[...]
    q = jnp.dot(x.astype(BF16), wq_ref[...], preferred_element_type=F32).astype(BF16)
    mk = mk_ref[...].astype(BF16)
    mv = mv_ref[...].astype(BF16)
    heads = []
    for h in range(MEM_HEADS):
        sl = slice(h * MEM_HD, (h + 1) * MEM_HD)
        s = _dot_nt(q[:, sl], mk[:, sl]) * (MEM_HD ** -0.5)
        m = jnp.max(s, axis=1, keepdims=True)
        e = jnp.exp(s - m)
        p = e / jnp.sum(e, axis=1, keepdims=True)
        heads.append(_dot(p, mv[:, sl]).astype(BF16))
    o = jnp.concatenate(heads, axis=1)
    att = jnp.dot(o, wo_ref[...], preferred_element_type=F32)
    o_ref[...] = _layer_norm(DN_ALPHA * x + att, g_ref[...], b_ref[...])


def _mem_attn_call(x3, wq, mk, mv, wo, g, b):
    bsz, t, _ = x3.shape
    tm = _row_tile(t, 512)
    row = pl.BlockSpec((None, tm, D_MODEL), lambda bi, i: (bi, i, 0))
    mem = pl.BlockSpec((None, MEM_TOKENS, MEM_W), lambda bi, i: (bi, 0, 0))
    full = lambda a: pl.BlockSpec(a.shape, lambda bi, i: (0,) * a.ndim)
    return pl.pallas_call(
        _mem_attn_kernel, grid=(bsz, t // tm),
        in_specs=[row, full(wq), mem, mem, full(wo), full(g), full(b)], out_specs=row,
        out_shape=jax.ShapeDtypeStruct((bsz, t, D_MODEL), F32),
        compiler_params=_cparams(("parallel", "parallel")), name="mem_attn")(x3, wq, mk, mv, wo, g, b)


def _shift_with_halo(u, halos, d, seg):
    rolled = pltpu.roll(u, d, 0)
    r8 = lax.broadcasted_iota(jnp.int32, halos[0].shape, 0)
    pieces = []
    for s, halo in enumerate(halos):
        lo = s * seg
        pieces.append(jnp.where(r8 < d, pltpu.roll(halo, d, 0), rolled[lo:lo + SUBLANES, :]))
        pieces.append(rolled[lo + SUBLANES:lo + seg, :])
    return jnp.concatenate(pieces, axis=0)


def _ffn_kernel(x_ref, xh_ref, prev_ref, wu_ref, cw_ref, cb_ref, wd_ref, g_ref, b_ref, o_ref, tail_ref,
                *, tm, seg, tiles_per_seq):
    x = x_ref[...]
    xb = x.astype(BF16)
    nseg = tm // seg
    c = FFN_CHUNK
    if tiles_per_seq:
        xhb = xh_ref[...].astype(BF16)
        first = (pl.program_id(0) % tiles_per_seq) == 0

    def up_proj(c0):
        w = wu_ref[:, c0:c0 + c]
        u = jnp.dot(xb, w, preferred_element_type=F32)
        uh = jnp.dot(xhb, w, preferred_element_type=F32) if tiles_per_seq else None
        return u, uh

    def conv(c0, u, uh):
        if tiles_per_seq:
            halos = [jnp.where(first, prev_ref[0, :, c0:c0 + c], uh)]
        else:
            halos = [prev_ref[s, :, c0:c0 + c] for s in range(nseg)]
        for s in range(nseg):
            tail_ref[s, :, c0:c0 + c] = u[(s + 1) * seg - SUBLANES:(s + 1) * seg, :]
        y = cb_ref[:, c0:c0 + c] + cw_ref[0:1, c0:c0 + c] * _shift_with_halo(u, halos, 2, seg)
        y = y + cw_ref[1:2, c0:c0 + c] * _shift_with_halo(u, halos, 1, seg)
        return y + cw_ref[2:3, c0:c0 + c] * u

    acc = None
    ups = (up_proj(0), up_proj(D_FF))
    for j in range(FFN_NCH):
        (ua, uha), (ug, uhg) = ups
        if j + 1 < FFN_NCH:
            ups = (up_proj((j + 1) * c), up_proj(D_FF + (j + 1) * c))
        a = conv(j * c, ua, uha)
        g = conv(D_FF + j * c, ug, uhg)
        hidden = (_gelu_tanh(a) * g).astype(BF16)
        part = jnp.dot(hidden, wd_ref[j * c:(j + 1) * c, :], preferred_element_type=F32)
        acc = part if acc is None else acc + part
    o_ref[...] = _layer_norm(DN_ALPHA * x + acc, g_ref[...], b_ref[...])


def _ffn_call(x3, w_up, conv_w, conv_b, w_down, prev8, g, b):
    bsz, t, _ = x3.shape
    n = bsz * t
    tm = min(FFN_TM, n)
    assert n % tm == 0 and (t % tm == 0 or tm % t == 0)
    seg = min(t, tm)
    nseg = tm // seg
    tiles_per_seq = t // tm
    x2 = x3.reshape(n, D_MODEL)
    row = pl.BlockSpec((tm, D_MODEL), lambda i: (i, 0))
    halo = pl.BlockSpec((SUBLANES, D_MODEL), lambda i: (jnp.maximum(i * (tm // SUBLANES) - 1, 0), 0))
    if tiles_per_seq:
        prev = pl.BlockSpec((1, SUBLANES, 2 * D_FF), lambda i: (i // tiles_per_seq, 0, 0))
    else:
        prev = pl.BlockSpec((nseg, SUBLANES, 2 * D_FF), lambda i: (i, 0, 0))
    resident = lambda a: pl.BlockSpec(a.shape, lambda i: (0,) * a.ndim, pipeline_mode=pl.Buffered(1))
    y, tail = pl.pallas_call(
        functools.partial(_ffn_kernel, tm=tm, seg=seg, tiles_per_seq=tiles_per_seq), grid=(n // tm,),
        in_specs=[row, halo, prev, resident(w_up), resident(conv_w), resident(conv_b), resident(w_down),
                  resident(g), resident(b)],
        out_specs=[row, pl.BlockSpec((None, nseg, SUBLANES, 2 * D_FF), lambda i: (i, 0, 0, 0))],
        out_shape=[jax.ShapeDtypeStruct((n, D_MODEL), F32),
                   jax.ShapeDtypeStruct((n // tm, nseg, SUBLANES, 2 * D_FF), F32)],
        compiler_params=_cparams(("parallel",)), name="conv_ffn")(
            x2, x2, prev8, w_up, conv_w, conv_b, w_down, g, b)
    if tiles_per_seq:
        tail = tail.reshape(bsz, tiles_per_seq, SUBLANES, 2 * D_FF)[:, -1]
    else:
        tail = tail.reshape(bsz, SUBLANES, 2 * D_FF)
    return y.reshape(bsz, t, D_MODEL), tail


def _block_diag(blocks):
    g, r, c = blocks.shape
    eye = jnp.eye(g, dtype=blocks.dtype)
    return (blocks[:, :, None, :] * eye[:, None, :, None]).reshape(g * r, g * c)


def _prep_layer(l, p):
    w = p['w_in'][l]
    offs = [0, 128, 256, 512, 768, 784, 1040, 1552, 2064, 2576, 2584]
    gq, gk, gv, go, ga, su, fq, fk, fv, ff = [w[:, offs[n]:offs[n + 1]] for n in range(10)]
    pad = jnp.zeros((D_MODEL, LANES - FOX_HEADS - GLA_RANK), F32)
    w_r = jnp.concatenate([gq, gk, gv, go, su, fq, fk, fv, ff, ga, pad], axis=1).astype(BF16)
    pad_heads = lambda m: jnp.pad(m.reshape(D_MODEL, FOX_HEADS, FOX_HD), ((0, 0), (0, 0), (0, LANES - FOX_HD))
                                  ).reshape(D_MODEL, FOX_AUG_W)
    wqt_pad = pad_heads(fq).T.astype(BF16)
    wk_pad = pad_heads(fk).astype(BF16)
    wvt = fv.T.astype(BF16)
    wa_p = jnp.zeros((LANES, GLA_QK), F32).at[SM_GA:SM_GA + GLA_RANK].set(p['gla_w_a2'][l]).astype(BF16)
    row = lambda a: a.reshape(1, -1)
    flat = lambda a: a.reshape(1, S5_N)
    b_t = lambda a: jnp.transpose(a, (2, 0, 1)).reshape(S5_CH, S5_N)
    pre, pim, bb_re, bb_im = _s5_prep_call(flat(p['s5_lam_re'][l]), flat(p['s5_lam_im'][l]),
                                           flat(p['s5_log_dt'][l]), b_t(p['s5_b_re'][l]), b_t(p['s5_b_im'][l]))
    bd_in = lambda bb: _block_diag(jnp.transpose(bb.reshape(S5_CH, S5_GROUPS, S5_STATE), (1, 0, 2)))
    wbu = jnp.concatenate([bd_in(bb_re), bd_in(bb_im)], axis=1).astype(BF16)
    bd_out = lambda cm: _block_diag(jnp.transpose(cm, (0, 2, 1))).astype(BF16)
    return dict(
        w_r=w_r, wqt_pad=wqt_pad, wk_pad=wk_pad, wvt=wvt,
        b_f=row(p['fox_b_f'][l]), wa_p=wa_p, ba=row(p['gla_b_a'][l]),
        ng4=jnp.tile(p['gla_norm_g'][l], GLA_HEADS).reshape(1, GLA_W),
        wbu=wbu, pre=pre, pim=pim, wcr=bd_out(p['s5_c_re'][l]), wci=bd_out(p['s5_c_im'][l]),
        d_skip=row(p['s5_d'][l]), wg=p['s5_w_glu'][l].astype(BF16), bg=row(p['s5_b_glu'][l]),
        w_mix=p['w_mix_out'][l].astype(BF16), ln1_g=row(p['ln1_g'][l]), ln1_b=row(p['ln1_b'][l]),
        wq=p['mem_w_q'][l].astype(BF16), wk=p['mem_w_k'][l].astype(BF16), wv=p['mem_w_v'][l].astype(BF16),
        wo=p['mem_w_o'][l].astype(BF16), ln2_g=row(p['ln2_g'][l]), ln2_b=row(p['ln2_b'][l]),
        w_up=p['ffn_w_up'][l].astype(BF16), conv_w=p['ffn_conv_w'][l], conv_b=row(p['ffn_conv_b'][l]),
        w_down=p['ffn_w_down'][l].astype(BF16), ln3_g=row(p['ln3_g'][l]), ln3_b=row(p['ln3_b'][l]))


def _gla_state_in(s0):
    eye = jnp.eye(GLA_HEADS, dtype=s0.dtype)
    st = jnp.transpose(s0, (0, 1, 3, 2))
    full = st[:, :, :, None, :] * eye[None, :, None, :, None]
    return full.reshape(s0.shape[0], GLA_W, GLA_QK)


def _gla_state_out(st):
    s5 = st.reshape(st.shape[0], GLA_HEADS, GLA_DV, GLA_HEADS, GLA_DK)
    diag = jnp.stack([s5[:, h, :, h, :] for h in range(GLA_HEADS)], axis=1)
    return jnp.transpose(diag, (0, 1, 3, 2))


def _layer(x3, mem_k, mem_v, gla_s0, s5_h0_re, s5_h0_im, past_k, past_v, past_logf, conv_prev, lp):
    bsz, t, _ = x3.shape
    n = bsz * t
    x2 = x3.reshape(n, D_MODEL)
    gqk, gv, go, su, fk, fv, sm, logf3, *fox_ops = _in_proj_call(x3, lp, fox_t=past_k is None)

    o_gla, st_t = _gla_call(gqk, gv, go, sm, lp['wa_p'], lp['ba'], lp['ng4'], _gla_state_in(gla_s0))
    o_s5, h_re, h_im = _s5_call(su, lp['wbu'], lp['pre'], lp['pim'], lp['wcr'], lp['wci'], lp['d_skip'],
                                lp['wg'], lp['bg'], s5_h0_re.reshape(bsz, 1, S5_N), s5_h0_im.reshape(bsz, 1, S5_N))

    if past_k is None:
        qt_pad, k_pad, vt = fox_ops
        k_aug, qt_aug = _fox_aug_call(logf3, k_pad, qt_pad)
        o_fox = _flash_t_call(qt_aug, k_aug, vt)
    else:
        past = past_k.shape[1]
        k_all = jnp.concatenate([past_k.reshape(bsz, past, FOX_W), fk], axis=1)
        v_all = jnp.concatenate([past_v.reshape(bsz, past, FOX_W), fv], axis=1)
        cum_f = _cumsum_call(jnp.concatenate([past_logf, logf3], axis=1))
        o_fox = _flash_call(fox_ops[0], k_all, v_all, cum_f[:, past:], jnp.swapaxes(cum_f, 1, 2), past)

    x1 = _mix_ln_call(o_gla.reshape(n, GLA_W), o_s5.reshape(n, S5_W), o_fox.reshape(n, FOX_W), x2,
                      lp['w_mix'], lp['ln1_g'], lp['ln1_b'])
    x2b = _mem_attn_call(x1.reshape(bsz, t, D_MODEL), lp['wq'], mem_k, mem_v, lp['wo'], lp['ln2_g'], lp['ln2_b'])
    prev8 = jnp.concatenate([jnp.zeros((bsz, SUBLANES - (CONV_W - 1), 2 * D_FF), F32), conv_prev], axis=1)
    y3, tail = _ffn_call(x2b, lp['w_up'], lp['conv_w'], lp['conv_b'], lp['w_down'], prev8,
                         lp['ln3_g'], lp['ln3_b'])
    conv_new = tail[:, SUBLANES - (CONV_W - 1):, :]
    states = (_gla_state_out(st_t), h_re.reshape(bsz, S5_GROUPS, S5_STATE), h_im.reshape(bsz, S5_GROUPS, S5_STATE),
              fk.reshape(bsz, t, FOX_HEADS, FOX_HD), fv.reshape(bsz, t, FOX_HEADS, FOX_HD), logf3, conv_new)
    return y3, states


def kernel(x_prompt, x_sample, mem_prompt, state_gla, state_s5_re, state_s5_im, cache_fox_k, cache_fox_v,
           cache_fox_logf, cache_mem_k, cache_mem_v, state_ffn_conv, ln_in_g, ln_in_b, w_in, gla_w_a2, gla_b_a,
           gla_norm_g, s5_lam_re, s5_lam_im, s5_log_dt, s5_b_re, s5_b_im, s5_c_re, s5_c_im, s5_d, s5_w_glu,
           s5_b_glu, fox_b_f, w_mix_out, ln1_g, ln1_b, mem_w_q, mem_w_k, mem_w_v, mem_w_o, ln2_g, ln2_b,
           ffn_w_up, ffn_conv_w, ffn_conv_b, ffn_w_down, ln3_g, ln3_b):
    params = dict(w_in=w_in, gla_w_a2=gla_w_a2, gla_b_a=gla_b_a, gla_norm_g=gla_norm_g, s5_lam_re=s5_lam_re,
                  s5_lam_im=s5_lam_im, s5_log_dt=s5_log_dt, s5_b_re=s5_b_re, s5_b_im=s5_b_im, s5_c_re=s5_c_re,
                  s5_c_im=s5_c_im, s5_d=s5_d, s5_w_glu=s5_w_glu, s5_b_glu=s5_b_glu, fox_b_f=fox_b_f,
                  w_mix_out=w_mix_out, ln1_g=ln1_g, ln1_b=ln1_b, mem_w_q=mem_w_q, mem_w_k=mem_w_k,
                  mem_w_v=mem_w_v, mem_w_o=mem_w_o, ln2_g=ln2_g, ln2_b=ln2_b, ffn_w_up=ffn_w_up,
                  ffn_conv_w=ffn_conv_w, ffn_conv_b=ffn_conv_b, ffn_w_down=ffn_w_down, ln3_g=ln3_g, ln3_b=ln3_b)
    bp, tp, _ = x_prompt.shape
    bs, ts, _ = x_sample.shape
    g_in, b_in = ln_in_g.reshape(1, D_MODEL), ln_in_b.reshape(1, D_MODEL)
    hp = _ln_call(x_prompt.reshape(bp * tp, D_MODEL), g_in, b_in).reshape(bp, tp, D_MODEL)
    hs = _ln_call(x_sample.reshape(bs * ts, D_MODEL), g_in, b_in).reshape(bs, ts, D_MODEL)
    mem2 = mem_prompt.reshape(bp * MEM_TOKENS, D_MODEL)
    zero_gla = jnp.zeros((bp, GLA_HEADS, GLA_DK, GLA_DV), F32)
    zero_s5 = jnp.zeros((bp, S5_GROUPS, S5_STATE), F32)
    zero_conv = jnp.zeros((bp, CONV_W - 1, 2 * D_FF), F32)
    prompt_states, sample_states = [], []
    for l in range(DEPTH):
        lp = _prep_layer(l, params)
        mk2, mv2 = _mem_kv_call(mem2, lp['wk'], lp['wv'])
        mk_p = mk2.reshape(bp, MEM_TOKENS, MEM_W)
        mv_p = mv2.reshape(bp, MEM_TOKENS, MEM_W)
        hp, st_p = _layer(hp, mk_p, mv_p, zero_gla, zero_s5, zero_s5, None, None, None, zero_conv, lp)
        prompt_states.append(st_p + (mk_p.reshape(bp, MEM_TOKENS, MEM_HEADS, MEM_HD),
                                     mv_p.reshape(bp, MEM_TOKENS, MEM_HEADS, MEM_HD)))
        hs, st_s = _layer(hs, cache_mem_k[l].reshape(bs, MEM_TOKENS, MEM_W),
                          cache_mem_v[l].reshape(bs, MEM_TOKENS, MEM_W), state_gla[l], state_s5_re[l],
                          state_s5_im[l], cache_fox_k[l], cache_fox_v[l], cache_fox_logf[l], state_ffn_conv[l], lp)
        sample_states.append(st_s)
    p_out = [jnp.stack(z) for z in zip(*prompt_states)]
    s_out = [jnp.stack(z) for z in zip(*sample_states)]
    return (hp, hs, *p_out, *s_out)
```

```python
import functools
import math

import jax
import jax.numpy as jnp
from jax import lax
from jax.experimental import pallas as pl
from jax.experimental.pallas import tpu as pltpu

F32 = jnp.float32
BF16 = jnp.bfloat16

D_MODEL = 1024
DEPTH = 2
CHUNK = 64
GLA_HEADS = 4
GLA_DK = 32
GLA_DV = 64
GLA_RANK = 16
GLA_TAU = 16.0
S5_GROUPS = 16
S5_CH = 16
S5_STATE = 64
FOX_HEADS = 8
FOX_HD = 64
MEM_TOKENS = 256
MEM_HEADS = 4
MEM_HD = 256
D_FF = 2816
CONV_W = 3
LN_EPS = 1e-5
DN_ALPHA = (2 * DEPTH) ** 0.25

GLA_QK = GLA_HEADS * GLA_DK
GLA_W = GLA_HEADS * GLA_DV
S5_W = S5_GROUPS * S5_CH
S5_N = S5_GROUPS * S5_STATE
FOX_W = FOX_HEADS * FOX_HD
MEM_W = MEM_HEADS * MEM_HD

LANES = 128
SUBLANES = 8
VMEM_LIMIT = 56 * 1024 * 1024

C_GQK = 0
C_GV = 256
C_GO = 512
C_SU = 768
C_FQ = 1024
C_FK = 1536
C_FV = 2048
C_SM = 2560
N_INR = 2688
SM_FF = 0
SM_GA = FOX_HEADS

FFN_CHUNK = 256
FFN_NCH = D_FF // FFN_CHUNK
FFN_TM = 512
S5_TILE = 256
FOX_SCALE = FOX_HD ** -0.5
LOG2E = math.log2(math.e)
FOX_AUG_W = FOX_HEADS * LANES
FOX_NPIECE = 3
FOX_TQ = 512
FOX_TK = 512


def _cparams(sem):
    return pltpu.CompilerParams(dimension_semantics=sem, vmem_limit_bytes=VMEM_LIMIT)


def _dot(a, b):
    return jnp.dot(a.astype(BF16), b.astype(BF16), preferred_element_type=F32)


def _dot_nt(a, b):
    return lax.dot_general(a.astype(BF16), b.astype(BF16), (((1,), (1,)), ((), ())),
                           preferred_element_type=F32)


def _dot_tn(a, b):
    return lax.dot_general(a.astype(BF16), b.astype(BF16), (((0,), (0,)), ((), ())),
                           preferred_element_type=F32)


def _split3(x):
    hi = x.astype(BF16)
    r1 = x - hi.astype(F32)
    mid = r1.astype(BF16)
    lo = (r1 - mid.astype(F32)).astype(BF16)
    return hi, mid, lo


def _tri(n):
    r = lax.broadcasted_iota(jnp.int32, (n, n), 0)
    c = lax.broadcasted_iota(jnp.int32, (n, n), 1)
    return (c <= r).astype(BF16)


def _cumsum_rows(x, tri):
    hi, mid, lo = _split3(x)
    d = lambda p: jnp.dot(tri, p, preferred_element_type=F32)
    return d(hi) + (d(mid) + d(lo))


def _layer_norm(x, g, b):
    mu = jnp.mean(x, axis=-1, keepdims=True)
    xc = x - mu
    var = jnp.mean(xc * xc, axis=-1, keepdims=True)
    return xc * lax.rsqrt(var + LN_EPS) * g + b


def _log_sigmoid(x):
    return jnp.minimum(x, 0.0) - jnp.log1p(jnp.exp(-jnp.abs(x)))


def _sigmoid(x):
    return 1.0 / (1.0 + jnp.exp(-x))


def _gelu_tanh(x):
    c = math.sqrt(2.0 / math.pi)
    return 0.5 * x * (1.0 + jnp.tanh(c * (x + 0.044715 * (x * x * x))))


def _row_tile(n, want):
    t = min(n, want)
    assert n % t == 0
    return t


def _ln_kernel(x_ref, g_ref, b_ref, o_ref):
    o_ref[...] = _layer_norm(x_ref[...], g_ref[...], b_ref[...])


def _ln_call(x2, g, b):
    n = x2.shape[0]
    tm = _row_tile(n, 512)
    row = pl.BlockSpec((tm, D_MODEL), lambda i: (i, 0))
    vec = pl.BlockSpec((1, D_MODEL), lambda i: (0, 0))
    return pl.pallas_call(
        _ln_kernel, grid=(n // tm,), in_specs=[row, vec, vec], out_specs=row,
        out_shape=jax.ShapeDtypeStruct((n, D_MODEL), F32),
        compiler_params=_cparams(("parallel",)), name="ln_in")(x2, g, b)


def _in_proj_kernel(x_ref, w_ref, bf_ref, gqk_ref, gv_ref, go_ref, su_ref, fk_ref, fv_ref, sm_ref, logf_ref,
                    fq_ref):
    xb = x_ref[...].astype(BF16)
    proj = lambda lo, hi: jnp.dot(xb, w_ref[:, lo:hi], preferred_element_type=F32)
    gqk_ref[...] = proj(C_GQK, C_GV)
    gv_ref[...] = proj(C_GV, C_GO)
    go_ref[...] = proj(C_GO, C_SU)
    su_ref[...] = proj(C_SU, C_FQ)
    fk_ref[...] = proj(C_FK, C_FV)
    fv_ref[...] = proj(C_FV, C_SM)
    sm = proj(C_SM, N_INR)
    sm_ref[...] = sm
    logf_ref[...] = _log_sigmoid(sm[:, SM_FF:SM_FF + FOX_HEADS] + bf_ref[...])
    fq_ref[...] = (proj(C_FQ, C_FK) * FOX_SCALE).astype(BF16)


def _in_proj_t_kernel(x_ref, w_ref, bf_ref, wqt_ref, wkp_ref, wvt_ref, gqk_ref, gv_ref, go_ref, su_ref, fk_ref,
                      fv_ref, sm_ref, logf_ref, qt_ref, kp_ref, vt_ref):
    xb = x_ref[...].astype(BF16)
    proj = lambda lo, hi: jnp.dot(xb, w_ref[:, lo:hi], preferred_element_type=F32)
    gqk_ref[...] = proj(C_GQK, C_GV)
    gv_ref[...] = proj(C_GV, C_GO)
    go_ref[...] = proj(C_GO, C_SU)
    su_ref[...] = proj(C_SU, C_FQ)
    fk_ref[...] = proj(C_FK, C_FV)
    fv_ref[...] = proj(C_FV, C_SM)
    sm = proj(C_SM, N_INR)
    sm_ref[...] = sm
    logf_ref[...] = _log_sigmoid(sm[:, SM_FF:SM_FF + FOX_HEADS] + bf_ref[...])
    qt_ref[...] = (_dot_nt(wqt_ref[...], xb) * (FOX_SCALE * LOG2E)).astype(BF16)
    kp_ref[...] = jnp.dot(xb, wkp_ref[...], preferred_element_type=F32).astype(BF16)
    vt_ref[...] = _dot_nt(wvt_ref[...], xb).astype(BF16)


def _in_proj_call(x3, lp, fox_t):
    bsz, t, _ = x3.shape
    tm = _row_tile(t, 512)
    row = lambda w: pl.BlockSpec((None, tm, w), lambda bi, i: (bi, i, 0))
    col = lambda r: pl.BlockSpec((None, r, tm), lambda bi, i: (bi, 0, i))
    full = lambda a: pl.BlockSpec(a.shape, lambda bi, i: (0,) * a.ndim)
    sds = lambda w, dt: jax.ShapeDtypeStruct((bsz, t, w), dt)
    in_arrays = [x3, lp['w_r'], lp['b_f']]
    out_specs = [row(256), row(256), row(256), row(256), row(FOX_W), row(FOX_W), row(LANES), row(FOX_HEADS)]
    out_shape = [sds(256, F32), sds(256, F32), sds(256, F32), sds(256, F32), sds(FOX_W, F32), sds(FOX_W, F32),
                 sds(LANES, F32), sds(FOX_HEADS, F32)]
    if fox_t:
        body = _in_proj_t_kernel
        in_arrays += [lp['wqt_pad'], lp['wk_pad'], lp['wvt']]
        out_specs += [col(FOX_AUG_W), row(FOX_AUG_W), col(FOX_W)]
        out_shape += [jax.ShapeDtypeStruct((bsz, FOX_AUG_W, t), BF16), sds(FOX_AUG_W, BF16),
                      jax.ShapeDtypeStruct((bsz, FOX_W, t), BF16)]
    else:
        body = _in_proj_kernel
        out_specs += [row(FOX_W)]
        out_shape += [sds(FOX_W, BF16)]
    in_specs = [row(D_MODEL)] + [full(a) for a in in_arrays[1:]]
    return pl.pallas_call(
        body, grid=(bsz, t // tm), in_specs=in_specs, out_specs=out_specs, out_shape=out_shape,
        compiler_params=_cparams(("parallel", "parallel")), name="in_proj")(*in_arrays)


def _fox_aug_kernel(logf_ref, kp_ref, qt_ref, ko_ref, qo_ref, carry_sc, *, tt):
    @pl.when(pl.program_id(1) == 0)
    def _():
        carry_sc[...] = jnp.zeros(carry_sc.shape, F32)

    cum = _cumsum_rows(logf_ref[...], _tri(tt)) + carry_sc[...]
    carry_sc[...] = cum[tt - 1:tt, :]
    pieces = jnp.concatenate([p.astype(F32) for p in _split3(cum * LOG2E)], axis=1)
    npc = FOX_NPIECE * FOX_HEADS

    lane = lax.broadcasted_iota(jnp.int32, (npc, FOX_AUG_W), 1)
    src_k = lax.broadcasted_iota(jnp.int32, (npc, FOX_AUG_W), 0)
    place_k = lane == (src_k % FOX_HEADS) * LANES + FOX_HD + FOX_NPIECE + src_k // FOX_HEADS
    aug_k = -jnp.dot(pieces, place_k.astype(F32), preferred_element_type=F32)
    off_k = lax.broadcasted_iota(jnp.int32, (1, FOX_AUG_W), 1) & (LANES - 1)
    ones_k = jnp.logical_and(off_k >= FOX_HD, off_k < FOX_HD + FOX_NPIECE).astype(F32)
    ko_ref[...] = (kp_ref[...].astype(F32) + aug_k + ones_k).astype(BF16)

    row = lax.broadcasted_iota(jnp.int32, (FOX_AUG_W, npc), 0)
    src_q = lax.broadcasted_iota(jnp.int32, (FOX_AUG_W, npc), 1)
    place_q = row == (src_q % FOX_HEADS) * LANES + FOX_HD + src_q // FOX_HEADS
    aug_q = lax.dot_general(place_q.astype(F32), pieces, (((1,), (1,)), ((), ())), preferred_element_type=F32)
    off_q = lax.broadcasted_iota(jnp.int32, (FOX_AUG_W, 1), 0) & (LANES - 1)
    ones_q = jnp.logical_and(off_q >= FOX_HD + FOX_NPIECE, off_q < FOX_HD + 2 * FOX_NPIECE).astype(F32)
    qo_ref[...] = (qt_ref[...].astype(F32) + aug_q + ones_q).astype(BF16)


def _fox_aug_call(logf3, k_pad, qt_pad):
    bsz, t, _ = logf3.shape
    tt = _row_tile(t, 512)
    kspec = pl.BlockSpec((None, tt, FOX_AUG_W), lambda bi, i: (bi, i, 0))
    qspec = pl.BlockSpec((None, FOX_AUG_W, tt), lambda bi, i: (bi, 0, i))
    return pl.pallas_call(
        functools.partial(_fox_aug_kernel, tt=tt), grid=(bsz, t // tt),
        in_specs=[pl.BlockSpec((None, tt, FOX_HEADS), lambda bi, i: (bi, i, 0)), kspec, qspec],
        out_specs=[kspec, qspec],
        out_shape=[jax.ShapeDtypeStruct(k_pad.shape, BF16), jax.ShapeDtypeStruct(qt_pad.shape, BF16)],
        scratch_shapes=[pltpu.VMEM((1, FOX_HEADS), F32)],
        input_output_aliases={1: 0, 2: 1},
        compiler_params=_cparams(("parallel", "arbitrary")), name="fox_aug")(logf3, k_pad, qt_pad)


def _flash_t_kernel(qi_ref, kj_ref, last_ref, qt_ref, k_ref, vt_ref, o_ref, m_sc, l_sc, acc_sc, *, tq, tk):
    step = pl.program_id(1)
    i = qi_ref[step]
    j = kj_ref[step]

    @pl.when(j == 0)
    def _():
        m_sc[...] = jnp.full(m_sc.shape, -jnp.inf, F32)
        l_sc[...] = jnp.zeros(l_sc.shape, F32)
        acc_sc[...] = jnp.zeros(acc_sc.shape, F32)

    def compute(masked):
        if masked:
            kpos = j * tk + lax.broadcasted_iota(jnp.int32, (tk, tq), 0)
            qpos = i * tq + lax.broadcasted_iota(jnp.int32, (tk, tq), 1)
            visible = kpos <= qpos
        scores = lambda h: jnp.dot(k_ref[:, h * LANES:(h + 1) * LANES], qt_ref[h * LANES:(h + 1) * LANES, :],
                                   preferred_element_type=F32)
        def max_pass(h, s_t):
            if masked:
                s_t = jnp.where(visible, s_t, -jnp.inf)
            m_prev = m_sc[h:h + 1, :]
            return s_t, m_prev, jnp.maximum(m_prev, jnp.max(s_t, axis=0, keepdims=True))

        s_ahead = scores(1)
        nxt = max_pass(0, scores(0))
        for h in range(FOX_HEADS):
            s_t, m_prev, m_new = nxt
            s_mid = s_ahead
            if h + 2 < FOX_HEADS:
                s_ahead = scores(h + 2)
            if h + 1 < FOX_HEADS:
                nxt = max_pass(h + 1, s_mid)
            alpha = jnp.exp2(m_prev - m_new)
            p_t = jnp.exp2(s_t - m_new)
            l_sc[h:h + 1, :] = alpha * l_sc[h:h + 1, :] + jnp.sum(p_t, axis=0, keepdims=True)
            rows = slice(h * FOX_HD, (h + 1) * FOX_HD)
            acc_sc[rows, :] = acc_sc[rows, :] * alpha + jnp.dot(vt_ref[rows, :], p_t.astype(BF16),
                                                                preferred_element_type=F32)
            m_sc[h:h + 1, :] = m_new

    fully_visible = j * tk + (tk - 1) <= i * tq

    @pl.when(fully_visible)
    def _():
        compute(False)

    @pl.when(jnp.logical_not(fully_visible))
    def _():
        compute(True)

    @pl.when(last_ref[step] == 1)
    def _():
        for h in range(FOX_HEADS):
            rows = slice(h * FOX_HD, (h + 1) * FOX_HD)
            acc_sc[rows, :] = acc_sc[rows, :] / l_sc[h:h + 1, :]
        o_ref[...] = acc_sc[...].T.astype(o_ref.dtype)


def _flash_t_call(qt_aug, k_aug, vt):
    bsz, _, t = qt_aug.shape
    tq, tk = _row_tile(t, FOX_TQ), _row_tile(t, FOX_TK)
    pairs = [(i, j) for i in range(t // tq) for j in range((i * tq + tq - 1) // tk + 1)]
    qi = jnp.asarray([p[0] for p in pairs], jnp.int32)
    kj = jnp.asarray([p[1] for p in pairs], jnp.int32)
    last = jnp.asarray([int(n + 1 == len(pairs) or pairs[n + 1][0] != p[0]) for n, p in enumerate(pairs)],
                       jnp.int32)
    grid_spec = pltpu.PrefetchScalarGridSpec(
        num_scalar_prefetch=3, grid=(bsz, len(pairs)),
        in_specs=[pl.BlockSpec((None, FOX_AUG_W, tq), lambda bi, s, qi, kj, last: (bi, 0, qi[s])),
                  pl.BlockSpec((None, tk, FOX_AUG_W), lambda bi, s, qi, kj, last: (bi, kj[s], 0)),
                  pl.BlockSpec((None, FOX_W, tk), lambda bi, s, qi, kj, last: (bi, 0, kj[s]))],
        out_specs=pl.BlockSpec((None, tq, FOX_W), lambda bi, s, qi, kj, last: (bi, qi[s], 0)),
        scratch_shapes=[pltpu.VMEM((FOX_HEADS, tq), F32), pltpu.VMEM((FOX_HEADS, tq), F32),
                        pltpu.VMEM((FOX_W, tq), F32)])
    return pl.pallas_call(
        functools.partial(_flash_t_kernel, tq=tq, tk=tk), grid_spec=grid_spec,
        out_shape=jax.ShapeDtypeStruct((bsz, t, FOX_W), BF16),
        compiler_params=_cparams(("parallel", "arbitrary")), name="fox_flash_t")(qi, kj, last, qt_aug, k_aug, vt)


def _exact_t(x):
    n = x.shape[1]
    eye = (lax.broadcasted_iota(jnp.int32, (n, n), 0) == lax.broadcasted_iota(jnp.int32, (n, n), 1)).astype(F32)
    mv = lambda p: lax.dot_general(eye, p.astype(F32), (((1,), (1,)), ((), ())), preferred_element_type=F32)
    hi, mid, lo = _split3(x)
    return mv(hi) + (mv(mid) + mv(lo))


def _fox_sample_kernel(q_ref, pk_ref, pv_ref, plf_ref, nk_ref, nv_ref, nlf_ref, o_ref, *, past, t, blk):
    tri = _tri(blk)
    carry = jnp.zeros((1, FOX_HEADS), F32)
    cf_p = []
    for c in range(past // blk):
        cs = _cumsum_rows(plf_ref[c * blk:(c + 1) * blk, :], tri) + carry
        cf_p.append(cs)
        carry = cs[blk - 1:blk, :]
    cf_p = jnp.concatenate(cf_p, axis=0)
    cf_n = _cumsum_rows(nlf_ref[...], _tri(t)) + carry
    cf_p_t = _exact_t(cf_p)
    cf_n_t = _exact_t(cf_n)
    causal = lax.broadcasted_iota(jnp.int32, (t, t), 1) <= lax.broadcasted_iota(jnp.int32, (t, t), 0)
    for h in range(FOX_HEADS):
        sl = slice(h * FOX_HD, (h + 1) * FOX_HD)
        qh = q_ref[:, sl]
        fq = cf_n[:, h:h + 1]
        s_p = _dot_nt(qh, pk_ref[:, sl]) + fq - cf_p_t[h:h + 1, :]
        s_n = _dot_nt(qh, nk_ref[:, sl]) + fq - cf_n_t[h:h + 1, :]
        s_n = jnp.where(causal, s_n, -jnp.inf)
        m = jnp.maximum(jnp.max(s_p, axis=1, keepdims=True), jnp.max(s_n, axis=1, keepdims=True))
        p_p = jnp.exp(s_p - m)
        p_n = jnp.exp(s_n - m)
        l = jnp.sum(p_p, axis=1, keepdims=True) + jnp.sum(p_n, axis=1, keepdims=True)
        acc = _dot(p_p, pv_ref[:, sl]) + _dot(p_n, nv_ref[:, sl])
        o_ref[:, sl] = (acc / l).astype(o_ref.dtype)


def _fox_sample_call(q, past_k, past_v, past_logf, new_k, new_v, new_logf):
    b, t, _ = q.shape
    past = past_k.shape[1]
    blk = max(c for c in range(SUBLANES, 512 + 1, SUBLANES) if past % c == 0)
    spec = lambda n, w: pl.BlockSpec((None, n, w), lambda bi: (bi, 0, 0))
    return pl.pallas_call(
        functools.partial(_fox_sample_kernel, past=past, t=t, blk=blk), grid=(b,),
        in_specs=[spec(t, FOX_W), spec(past, FOX_W), spec(past, FOX_W), spec(past, FOX_HEADS),
                  spec(t, FOX_W), spec(t, FOX_W), spec(t, FOX_HEADS)],
        out_specs=spec(t, FOX_W), out_shape=jax.ShapeDtypeStruct((b, t, FOX_W), BF16),
        compiler_params=_cparams(("parallel",)), name="fox_sample")(
            q, past_k, past_v, past_logf, new_k, new_v, new_logf)


def _gla_kernel(gqk_ref, gv_ref, go_ref, sm_ref, wa_ref, ba_ref, ng_ref, s0_ref, o_ref, sT_ref, st_sc, *, tb):
    i = pl.program_id(1)

    @pl.when(i == 0)
    def _():
        st_sc[...] = s0_ref[...]

    nchunk = tb // CHUNK
    r_i = lax.broadcasted_iota(jnp.int32, (tb, tb), 0)
    c_i = lax.broadcasted_iota(jnp.int32, (tb, tb), 1)
    causal = jnp.logical_and(r_i // CHUNK == c_i // CHUNK, c_i <= r_i)
    tri = causal.astype(BF16)
    lane_k = lax.broadcasted_iota(jnp.int32, (1, GLA_QK), 1) // GLA_DK
    lane_v = lax.broadcasted_iota(jnp.int32, (1, GLA_W), 1) // GLA_DV
    row_v = lax.broadcasted_iota(jnp.int32, (GLA_W, GLA_QK), 0) // GLA_DV
    col_k = lax.broadcasted_iota(jnp.int32, (GLA_W, GLA_QK), 1) // GLA_DK
    blockdiag = (row_v == col_k).astype(F32)
    rr = lax.broadcasted_iota(jnp.int32, (GLA_W, GLA_W), 0) // GLA_DV
    cc = lax.broadcasted_iota(jnp.int32, (GLA_W, GLA_W), 1) // GLA_DV
    avg = jnp.where(rr == cc, 1.0 / GLA_DV, 0.0).astype(BF16)

    q = gqk_ref[:, 0:GLA_QK] * (GLA_DK ** -0.5)
    k = gqk_ref[:, GLA_QK:2 * GLA_QK]
    v = gv_ref[...]
    z = _dot(sm_ref[...], wa_ref[...]) + ba_ref[...]
    log_a = _log_sigmoid(z) / GLA_TAU
    cum = _cumsum_rows(log_a, tri)
    ends = [cum[(c + 1) * CHUNK - 1:(c + 1) * CHUNK, :] for c in range(nchunk)]
    cum_end = jnp.concatenate([jnp.broadcast_to(e, (CHUNK, GLA_QK)) for e in ends], axis=0)
    q_dec = q * jnp.exp(cum)
    k_inv = k * jnp.exp(-cum)
    k_end = k * jnp.exp(cum_end - cum)
    o = jnp.zeros((tb, GLA_W), F32)
    for h in range(GLA_HEADS):
        qh = jnp.where(lane_k == h, q_dec, 0.0)
        sc = jnp.where(causal, _dot_nt(qh, k_inv), 0.0)
        o = o + jnp.where(lane_v == h, _dot(sc, v), 0.0)
    st = st_sc[...]
    inter = []
    for c in range(nchunk):
        rows = slice(c * CHUNK, (c + 1) * CHUNK)
        inter.append(_dot_nt(q_dec[rows, :], st))
        st = st * jnp.exp(ends[c]) + _dot_tn(v[rows, :], k_end[rows, :]) * blockdiag
    st_sc[...] = st
    sT_ref[...] = st
    o = o + jnp.concatenate(inter, axis=0)
    o2 = o * o
    hi = o2.astype(BF16)
    lo = (o2 - hi.astype(F32)).astype(BF16)
    ms = jnp.dot(hi, avg, preferred_element_type=F32) + jnp.dot(lo, avg, preferred_element_type=F32)
    o = o * lax.rsqrt(ms + LN_EPS) * ng_ref[...]
    g = go_ref[...]
    o_ref[...] = (o * (g * _sigmoid(g))).astype(o_ref.dtype)


def _gla_call(gqk, gv, go, sm, wa_p, ba, ng4, s0_t):
    b, t, _ = gqk.shape
    tb = _row_tile(t, 256)
    row = lambda w: pl.BlockSpec((None, tb, w), lambda bi, i: (bi, i, 0))
    full = lambda a: pl.BlockSpec(a.shape, lambda bi, i: (0,) * a.ndim)
    st_spec = pl.BlockSpec((None, GLA_W, GLA_QK), lambda bi, i: (bi, 0, 0))
    return pl.pallas_call(
        functools.partial(_gla_kernel, tb=tb), grid=(b, t // tb),
        in_specs=[row(2 * GLA_QK), row(GLA_W), row(GLA_W), row(LANES), full(wa_p), full(ba), full(ng4), st_spec],
        out_specs=[row(GLA_W), st_spec],
        out_shape=[jax.ShapeDtypeStruct((b, t, GLA_W), BF16), jax.ShapeDtypeStruct((b, GLA_W, GLA_QK), F32)],
        scratch_shapes=[pltpu.VMEM((GLA_W, GLA_QK), F32)],
        compiler_params=_cparams(("parallel", "arbitrary")), name="gla")(gqk, gv, go, sm, wa_p, ba, ng4, s0_t)


def _cmul(ar, ai, br, bi):
    return ar * br - ai * bi, ar * bi + ai * br


def _s5_prep_kernel(lre_ref, lim_ref, ldt_ref, bre_ref, bim_ref, pre_ref, pim_ref, bbre_ref, bbim_ref):
    lam_re, lam_im = lre_ref[...], lim_ref[...]
    dt = jnp.exp(ldt_ref[...])
    mag = jnp.exp(lam_re * dt)
    ab_re = mag * jnp.cos(lam_im * dt)
    ab_im = mag * jnp.sin(lam_im * dt)
    den = lam_re * lam_re + lam_im * lam_im
    f_re = ((ab_re - 1.0) * lam_re + ab_im * lam_im) / den
    f_im = (ab_im * lam_re - (ab_re - 1.0) * lam_im) / den
    b_re, b_im = bre_ref[...], bim_ref[...]
    bbre_ref[...] = f_re * b_re - f_im * b_im
    bbim_ref[...] = f_re * b_im + f_im * b_re
    pr = jnp.broadcast_to(ab_re, (S5_TILE, S5_N))
    pi = jnp.broadcast_to(ab_im, (S5_TILE, S5_N))
    r = lax.broadcasted_iota(jnp.int32, (S5_TILE, S5_N), 0)
    d = 1
    while d < S5_TILE:
        sr = jnp.where(r >= d, pltpu.roll(pr, d, 0), 1.0)
        si = jnp.where(r >= d, pltpu.roll(pi, d, 0), 0.0)
        pr, pi = _cmul(pr, pi, sr, si)
        d *= 2
    pre_ref[...] = pr
    pim_ref[...] = pi


def _s5_prep_call(lam_re, lam_im, log_dt, b_re_t, b_im_t):
    sds = jax.ShapeDtypeStruct
    return pl.pallas_call(
        _s5_prep_kernel,
        out_shape=[sds((S5_TILE, S5_N), F32), sds((S5_TILE, S5_N), F32),
                   sds((S5_CH, S5_N), F32), sds((S5_CH, S5_N), F32)],
        compiler_params=pltpu.CompilerParams(vmem_limit_bytes=VMEM_LIMIT), name="s5_prep")(
            lam_re, lam_im, log_dt, b_re_t, b_im_t)


def _s5_pitch(tt):
    steps = tt // SUBLANES
    assert steps % SUBLANES == 0
    return steps if (steps // SUBLANES) % 2 else steps + SUBLANES


def _s5_kernel(u_ref, wbu_ref, pre_ref, pim_ref, wcr_ref, wci_ref, d_ref, wg_ref, bg_ref, h0r_ref, h0i_ref,
               o_ref, hr_ref, hi_ref, cr_sc, ci_sc, xr_sc, xi_sc, *, tt):
    i = pl.program_id(1)
    steps = tt // SUBLANES

    @pl.when(i == 0)
    def _():
        cr_sc[...] = h0r_ref[...]
        ci_sc[...] = h0i_ref[...]

    u = u_ref[...]
    bu = _dot(u, wbu_ref[...])
    ntile = S5_N // LANES
    lanes = lambda j: slice(j * LANES, (j + 1) * LANES)
    pitch = _s5_pitch(tt)
    seg_rows = lambda k: slice(k * pitch, k * pitch + steps)
    for j in range(ntile):
        for k in range(SUBLANES):
            xr_sc[j, seg_rows(k), :] = bu[k * steps:(k + 1) * steps, j * LANES:(j + 1) * LANES]
            xi_sc[j, seg_rows(k), :] = bu[k * steps:(k + 1) * steps, S5_N + j * LANES:S5_N + (j + 1) * LANES]
    xr = [jnp.zeros((SUBLANES, LANES), F32) for _ in range(ntile)]
    xi = [jnp.zeros((SUBLANES, LANES), F32) for _ in range(ntile)]
    for t in range(steps):
        rows = pl.ds(t, SUBLANES, stride=pitch)
        for j in range(ntile):
            mr, mi = _cmul(pre_ref[0:1, lanes(j)], pim_ref[0:1, lanes(j)], xr[j], xi[j])
            xr[j], xi[j] = mr + xr_sc[j, rows, :], mi + xi_sc[j, rows, :]
            xr_sc[j, rows, :] = xr[j]
            xi_sc[j, rows, :] = xi[j]
    end_r, end_i = jnp.concatenate(xr, axis=1), jnp.concatenate(xi, axis=1)
    sr, si = pre_ref[steps - 1:steps, :], pim_ref[steps - 1:steps, :]
    c_r, c_i = cr_sc[...], ci_sc[...]
    ent_r, ent_i = [], []
    for k in range(SUBLANES):
        ent_r.append(c_r)
        ent_i.append(c_i)
        mr, mi = _cmul(sr, si, c_r, c_i)
        c_r, c_i = end_r[k:k + 1, :] + mr, end_i[k:k + 1, :] + mi
    cr_sc[...] = c_r
    ci_sc[...] = c_i
    hr_ref[...] = c_r
    hi_ref[...] = c_i
    ent_r = jnp.concatenate(ent_r, axis=0)
    ent_i = jnp.concatenate(ent_i, axis=0)
    for t in range(steps):
        rows = pl.ds(t, SUBLANES, stride=pitch)
        for j in range(ntile):
            mr, mi = _cmul(pre_ref[t:t + 1, lanes(j)], pim_ref[t:t + 1, lanes(j)],
                           ent_r[:, lanes(j)], ent_i[:, lanes(j)])
            xr_sc[j, rows, :] = xr_sc[j, rows, :] + mr
            xi_sc[j, rows, :] = xi_sc[j, rows, :] + mi
    gather = lambda sc: jnp.concatenate(
        [jnp.concatenate([sc[j, seg_rows(k), :] for k in range(SUBLANES)], axis=0) for j in range(ntile)], axis=1)
    x_re = gather(xr_sc)
    x_im = gather(xi_sc)
    y = _dot(x_re, wcr_ref[...]) - _dot(x_im, wci_ref[...]) + d_ref[...] * u
    z = _gelu_tanh(y)
    o_ref[...] = (z * _sigmoid(_dot(z, wg_ref[...]) + bg_ref[...])).astype(o_ref.dtype)


def _s5_call(u, wbu, pre, pim, wcr, wci, d_skip, wg, bg, h0r, h0i):
    b, t, _ = u.shape
    tt = _row_tile(t, S5_TILE)
    full = lambda a: pl.BlockSpec(a.shape, lambda bi, i: (0,) * a.ndim)
    pspec = pl.BlockSpec((tt, S5_N), lambda bi, i: (0, 0))
    hspec = pl.BlockSpec((None, 1, S5_N), lambda bi, i: (bi, 0, 0))
    row = pl.BlockSpec((None, tt, S5_W), lambda bi, i: (bi, i, 0))
    return pl.pallas_call(
        functools.partial(_s5_kernel, tt=tt), grid=(b, t // tt),
        in_specs=[row, full(wbu), pspec, pspec, full(wcr), full(wci), full(d_skip), full(wg), full(bg), hspec, hspec],
        out_specs=[row, hspec, hspec],
        out_shape=[jax.ShapeDtypeStruct((b, t, S5_W), BF16), jax.ShapeDtypeStruct((b, 1, S5_N), F32),
                   jax.ShapeDtypeStruct((b, 1, S5_N), F32)],
        scratch_shapes=[pltpu.VMEM((1, S5_N), F32), pltpu.VMEM((1, S5_N), F32),
                        pltpu.VMEM((S5_N // LANES, SUBLANES * _s5_pitch(tt), LANES), F32),
                        pltpu.VMEM((S5_N // LANES, SUBLANES * _s5_pitch(tt), LANES), F32)],
        compiler_params=_cparams(("parallel", "arbitrary")), name="s5")(
            u, wbu, pre, pim, wcr, wci, d_skip, wg, bg, h0r, h0i)


def _mix_ln_kernel(gla_ref, s5_ref, fox_ref, x_ref, w_ref, g_ref, b_ref, o_ref):
    mixed = jnp.dot(gla_ref[...], w_ref[0:GLA_W, :], preferred_element_type=F32)
    mixed = mixed + jnp.dot(s5_ref[...], w_ref[GLA_W:GLA_W + S5_W, :], preferred_element_type=F32)
    mixed = mixed + jnp.dot(fox_ref[...], w_ref[GLA_W + S5_W:, :], preferred_element_type=F32)
    o_ref[...] = _layer_norm(DN_ALPHA * x_ref[...] + mixed, g_ref[...], b_ref[...])


def _mix_ln_call(o_gla, o_s5, o_fox, x2, w, g, b):
    n = x2.shape[0]
    tm = _row_tile(n, 512)
    row = lambda w_: pl.BlockSpec((tm, w_), lambda i: (i, 0))
    full = lambda a: pl.BlockSpec(a.shape, lambda i: (0,) * a.ndim)
    return pl.pallas_call(
        _mix_ln_kernel, grid=(n // tm,),
        in_specs=[row(GLA_W), row(S5_W), row(FOX_W), row(D_MODEL), full(w), full(g), full(b)],
        out_specs=row(D_MODEL), out_shape=jax.ShapeDtypeStruct((n, D_MODEL), F32),
        compiler_params=_cparams(("parallel",)), name="mix_ln")(o_gla, o_s5, o_fox, x2, w, g, b)


def _mem_kv_kernel(m_ref, wk_ref, wv_ref, k_ref, v_ref):
    mb = m_ref[...].astype(BF16)
    k_ref[...] = jnp.dot(mb, wk_ref[...], preferred_element_type=F32)
    v_ref[...] = jnp.dot(mb, wv_ref[...], preferred_element_type=F32)


def _mem_kv_call(mem2, wk, wv):
    n = mem2.shape[0]
    tm = _row_tile(n, 256)
    row = pl.BlockSpec((tm, D_MODEL), lambda i: (i, 0))
    orow = pl.BlockSpec((tm, MEM_W), lambda i: (i, 0))
    full = lambda a: pl.BlockSpec(a.shape, lambda i: (0,) * a.ndim)
    sds = jax.ShapeDtypeStruct((n, MEM_W), F32)
    return pl.pallas_call(
        _mem_kv_kernel, grid=(n // tm,), in_specs=[row, full(wk), full(wv)], out_specs=[orow, orow],
        out_shape=[sds, sds], compiler_params=_cparams(("parallel",)), name="mem_kv")(mem2, wk, wv)


def _mem_attn_kernel(x_ref, wq_ref, mk_ref, mv_ref, wo_ref, g_ref, b_ref, o_ref):
    x = x_ref[...]
    q = jnp.dot(x.astype(BF16), wq_ref[...], preferred_element_type=F32).astype(BF16)
    mk = mk_ref[...].astype(BF16)
    mv = mv_ref[...].astype(BF16)
    heads = []
    for h in range(MEM_HEADS):
        sl = slice(h * MEM_HD, (h + 1) * MEM_HD)
        s = _dot_nt(q[:, sl], mk[:, sl]) * (MEM_HD ** -0.5)
        m = jnp.max(s, axis=1, keepdims=True)
        e = jnp.exp(s - m)
        p = e / jnp.sum(e, axis=1, keepdims=True)
        heads.append(_dot(p, mv[:, sl]).astype(BF16))
    o = jnp.concatenate(heads, axis=1)
    att = jnp.dot(o, wo_ref[...], preferred_element_type=F32)
    o_ref[...] = _layer_norm(DN_ALPHA * x + att, g_ref[...], b_ref[...])


def _mem_attn_call(x3, wq, mk, mv, wo, g, b):
    bsz, t, _ = x3.shape
    tm = _row_tile(t, 512)
    row = pl.BlockSpec((None, tm, D_MODEL), lambda bi, i: (bi, i, 0))
    mem = pl.BlockSpec((None, MEM_TOKENS, MEM_W), lambda bi, i: (bi, 0, 0))
    full = lambda a: pl.BlockSpec(a.shape, lambda bi, i: (0,) * a.ndim)
    return pl.pallas_call(
        _mem_attn_kernel, grid=(bsz, t // tm),
        in_specs=[row, full(wq), mem, mem, full(wo), full(g), full(b)], out_specs=row,
        out_shape=jax.ShapeDtypeStruct((bsz, t, D_MODEL), F32),
        compiler_params=_cparams(("parallel", "parallel")), name="mem_attn")(x3, wq, mk, mv, wo, g, b)


def _shift_with_halo(u, halos, d, seg):
    rolled = pltpu.roll(u, d, 0)
    r8 = lax.broadcasted_iota(jnp.int32, halos[0].shape, 0)
    pieces = []
    for s, halo in enumerate(halos):
        lo = s * seg
        pieces.append(jnp.where(r8 < d, pltpu.roll(halo, d, 0), rolled[lo:lo + SUBLANES, :]))
        pieces.append(rolled[lo + SUBLANES:lo + seg, :])
    return jnp.concatenate(pieces, axis=0)


def _ffn_kernel(x_ref, xh_ref, prev_ref, wu_ref, cw_ref, cb_ref, wd_ref, g_ref, b_ref, o_ref, tail_ref,
                *, tm, seg, tiles_per_seq):
    x = x_ref[...]
    xb = x.astype(BF16)
    nseg = tm // seg
    c = FFN_CHUNK
    if tiles_per_seq:
        xhb = xh_ref[...].astype(BF16)
        first = (pl.program_id(0) % tiles_per_seq) == 0

    def up_proj(c0):
        w = wu_ref[:, c0:c0 + c]
        u = jnp.dot(xb, w, preferred_element_type=F32)
        uh = jnp.dot(xhb, w, preferred_element_type=F32) if tiles_per_seq else None
        return u, uh

    def conv(c0, u, uh):
        if tiles_per_seq:
            halos = [jnp.where(first, prev_ref[0, :, c0:c0 + c], uh)]
        else:
            halos = [prev_ref[s, :, c0:c0 + c] for s in range(nseg)]
        for s in range(nseg):
            tail_ref[s, :, c0:c0 + c] = u[(s + 1) * seg - SUBLANES:(s + 1) * seg, :]
        y = cb_ref[:, c0:c0 + c] + cw_ref[0:1, c0:c0 + c] * _shift_with_halo(u, halos, 2, seg)
        y = y + cw_ref[1:2, c0:c0 + c] * _shift_with_halo(u, halos, 1, seg)
        return y + cw_ref[2:3, c0:c0 + c] * u

    acc = None
    ups = (up_proj(0), up_proj(D_FF))
    for j in range(FFN_NCH):
        (ua, uha), (ug, uhg) = ups
        if j + 1 < FFN_NCH:
            ups = (up_proj((j + 1) * c), up_proj(D_FF + (j + 1) * c))
        a = conv(j * c, ua, uha)
        g = conv(D_FF + j * c, ug, uhg)
        hidden = (_gelu_tanh(a) * g).astype(BF16)
        part = jnp.dot(hidden, wd_ref[j * c:(j + 1) * c, :], preferred_element_type=F32)
        acc = part if acc is None else acc + part
    o_ref[...] = _layer_norm(DN_ALPHA * x + acc, g_ref[...], b_ref[...])


def _ffn_call(x3, w_up, conv_w, conv_b, w_down, prev8, g, b):
    bsz, t, _ = x3.shape
    n = bsz * t
    tm = min(FFN_TM, n)
    assert n % tm == 0 and (t % tm == 0 or tm % t == 0)
    seg = min(t, tm)
    nseg = tm // seg
    tiles_per_seq = t // tm
    x2 = x3.reshape(n, D_MODEL)
    row = pl.BlockSpec((tm, D_MODEL), lambda i: (i, 0))
    halo = pl.BlockSpec((SUBLANES, D_MODEL), lambda i: (jnp.maximum(i * (tm // SUBLANES) - 1, 0), 0))
    if tiles_per_seq:
        prev = pl.BlockSpec((1, SUBLANES, 2 * D_FF), lambda i: (i // tiles_per_seq, 0, 0))
    else:
        prev = pl.BlockSpec((nseg, SUBLANES, 2 * D_FF), lambda i: (i, 0, 0))
    resident = lambda a: pl.BlockSpec(a.shape, lambda i: (0,) * a.ndim, pipeline_mode=pl.Buffered(1))
    y, tail = pl.pallas_call(
        functools.partial(_ffn_kernel, tm=tm, seg=seg, tiles_per_seq=tiles_per_seq), grid=(n // tm,),
        in_specs=[row, halo, prev, resident(w_up), resident(conv_w), resident(conv_b), resident(w_down),
                  resident(g), resident(b)],
        out_specs=[row, pl.BlockSpec((None, nseg, SUBLANES, 2 * D_FF), lambda i: (i, 0, 0, 0))],
        out_shape=[jax.ShapeDtypeStruct((n, D_MODEL), F32),
                   jax.ShapeDtypeStruct((n // tm, nseg, SUBLANES, 2 * D_FF), F32)],
        compiler_params=_cparams(("parallel",)), name="conv_ffn")(
            x2, x2, prev8, w_up, conv_w, conv_b, w_down, g, b)
    if tiles_per_seq:
        tail = tail.reshape(bsz, tiles_per_seq, SUBLANES, 2 * D_FF)[:, -1]
    else:
        tail = tail.reshape(bsz, SUBLANES, 2 * D_FF)
    return y.reshape(bsz, t, D_MODEL), tail


def _block_diag(blocks):
    g, r, c = blocks.shape
    eye = jnp.eye(g, dtype=blocks.dtype)
    return (blocks[:, :, None, :] * eye[:, None, :, None]).reshape(g * r, g * c)


def _prep_layer(l, p):
    w = p['w_in'][l]
    offs = [0, 128, 256, 512, 768, 784, 1040, 1552, 2064, 2576, 2584]
    gq, gk, gv, go, ga, su, fq, fk, fv, ff = [w[:, offs[n]:offs[n + 1]] for n in range(10)]
    pad = jnp.zeros((D_MODEL, LANES - FOX_HEADS - GLA_RANK), F32)
    w_r = jnp.concatenate([gq, gk, gv, go, su, fq, fk, fv, ff, ga, pad], axis=1).astype(BF16)
    pad_heads = lambda m: jnp.pad(m.reshape(D_MODEL, FOX_HEADS, FOX_HD), ((0, 0), (0, 0), (0, LANES - FOX_HD))
                                  ).reshape(D_MODEL, FOX_AUG_W)
    wqt_pad = pad_heads(fq).T.astype(BF16)
    wk_pad = pad_heads(fk).astype(BF16)
    wvt = fv.T.astype(BF16)
    wa_p = jnp.zeros((LANES, GLA_QK), F32).at[SM_GA:SM_GA + GLA_RANK].set(p['gla_w_a2'][l]).astype(BF16)
    row = lambda a: a.reshape(1, -1)
    flat = lambda a: a.reshape(1, S5_N)
    b_t = lambda a: jnp.transpose(a, (2, 0, 1)).reshape(S5_CH, S5_N)
    pre, pim, bb_re, bb_im = _s5_prep_call(flat(p['s5_lam_re'][l]), flat(p['s5_lam_im'][l]),
                                           flat(p['s5_log_dt'][l]), b_t(p['s5_b_re'][l]), b_t(p['s5_b_im'][l]))
    bd_in = lambda bb: _block_diag(jnp.transpose(bb.reshape(S5_CH, S5_GROUPS, S5_STATE), (1, 0, 2)))
    wbu = jnp.concatenate([bd_in(bb_re), bd_in(bb_im)], axis=1).astype(BF16)
    bd_out = lambda cm: _block_diag(jnp.transpose(cm, (0, 2, 1))).astype(BF16)
    return dict(
        w_r=w_r, wqt_pad=wqt_pad, wk_pad=wk_pad, wvt=wvt,
        b_f=row(p['fox_b_f'][l]), wa_p=wa_p, ba=row(p['gla_b_a'][l]),
        ng4=jnp.tile(p['gla_norm_g'][l], GLA_HEADS).reshape(1, GLA_W),
        wbu=wbu, pre=pre, pim=pim, wcr=bd_out(p['s5_c_re'][l]), wci=bd_out(p['s5_c_im'][l]),
        d_skip=row(p['s5_d'][l]), wg=p['s5_w_glu'][l].astype(BF16), bg=row(p['s5_b_glu'][l]),
        w_mix=p['w_mix_out'][l].astype(BF16), ln1_g=row(p['ln1_g'][l]), ln1_b=row(p['ln1_b'][l]),
        wq=p['mem_w_q'][l].astype(BF16), wk=p['mem_w_k'][l].astype(BF16), wv=p['mem_w_v'][l].astype(BF16),
        wo=p['mem_w_o'][l].astype(BF16), ln2_g=row(p['ln2_g'][l]), ln2_b=row(p['ln2_b'][l]),
        w_up=p['ffn_w_up'][l].astype(BF16), conv_w=p['ffn_conv_w'][l], conv_b=row(p['ffn_conv_b'][l]),
        w_down=p['ffn_w_down'][l].astype(BF16), ln3_g=row(p['ln3_g'][l]), ln3_b=row(p['ln3_b'][l]))


def _gla_state_in(s0):
    eye = jnp.eye(GLA_HEADS, dtype=s0.dtype)
    st = jnp.transpose(s0, (0, 1, 3, 2))
    full = st[:, :, :, None, :] * eye[None, :, None, :, None]
    return full.reshape(s0.shape[0], GLA_W, GLA_QK)


def _gla_state_out(st):
    s5 = st.reshape(st.shape[0], GLA_HEADS, GLA_DV, GLA_HEADS, GLA_DK)
    diag = jnp.stack([s5[:, h, :, h, :] for h in range(GLA_HEADS)], axis=1)
    return jnp.transpose(diag, (0, 1, 3, 2))


def _layer(x3, mem_k, mem_v, gla_s0, s5_h0_re, s5_h0_im, past_k, past_v, past_logf, conv_prev, lp):
    bsz, t, _ = x3.shape
    n = bsz * t
    x2 = x3.reshape(n, D_MODEL)
    gqk, gv, go, su, fk, fv, sm, logf3, *fox_ops = _in_proj_call(x3, lp, fox_t=past_k is None)

    o_gla, st_t = _gla_call(gqk, gv, go, sm, lp['wa_p'], lp['ba'], lp['ng4'], _gla_state_in(gla_s0))
    o_s5, h_re, h_im = _s5_call(su, lp['wbu'], lp['pre'], lp['pim'], lp['wcr'], lp['wci'], lp['d_skip'],
                                lp['wg'], lp['bg'], s5_h0_re.reshape(bsz, 1, S5_N), s5_h0_im.reshape(bsz, 1, S5_N))

    if past_k is None:
        qt_pad, k_pad, vt = fox_ops
        k_aug, qt_aug = _fox_aug_call(logf3, k_pad, qt_pad)
        o_fox = _flash_t_call(qt_aug, k_aug, vt)
    else:
        past = past_k.shape[1]
        o_fox = _fox_sample_call(fox_ops[0], past_k.reshape(bsz, past, FOX_W), past_v.reshape(bsz, past, FOX_W),
                                 past_logf, fk, fv, logf3)

    x1 = _mix_ln_call(o_gla.reshape(n, GLA_W), o_s5.reshape(n, S5_W), o_fox.reshape(n, FOX_W), x2,
                      lp['w_mix'], lp['ln1_g'], lp['ln1_b'])
    x2b = _mem_attn_call(x1.reshape(bsz, t, D_MODEL), lp['wq'], mem_k, mem_v, lp['wo'], lp['ln2_g'], lp['ln2_b'])
    prev8 = jnp.concatenate([jnp.zeros((bsz, SUBLANES - (CONV_W - 1), 2 * D_FF), F32), conv_prev], axis=1)
    y3, tail = _ffn_call(x2b, lp['w_up'], lp['conv_w'], lp['conv_b'], lp['w_down'], prev8,
                         lp['ln3_g'], lp['ln3_b'])
    conv_new = tail[:, SUBLANES - (CONV_W - 1):, :]
    states = (_gla_state_out(st_t), h_re.reshape(bsz, S5_GROUPS, S5_STATE), h_im.reshape(bsz, S5_GROUPS, S5_STATE),
              fk.reshape(bsz, t, FOX_HEADS, FOX_HD), fv.reshape(bsz, t, FOX_HEADS, FOX_HD), logf3, conv_new)
    return y3, states


def kernel(x_prompt, x_sample, mem_prompt, state_gla, state_s5_re, state_s5_im, cache_fox_k, cache_fox_v,
           cache_fox_logf, cache_mem_k, cache_mem_v, state_ffn_conv, ln_in_g, ln_in_b, w_in, gla_w_a2, gla_b_a,
           gla_norm_g, s5_lam_re, s5_lam_im, s5_log_dt, s5_b_re, s5_b_im, s5_c_re, s5_c_im, s5_d, s5_w_glu,
           s5_b_glu, fox_b_f, w_mix_out, ln1_g, ln1_b, mem_w_q, mem_w_k, mem_w_v, mem_w_o, ln2_g, ln2_b,
           ffn_w_up, ffn_conv_w, ffn_conv_b, ffn_w_down, ln3_g, ln3_b):
    params = dict(w_in=w_in, gla_w_a2=gla_w_a2, gla_b_a=gla_b_a, gla_norm_g=gla_norm_g, s5_lam_re=s5_lam_re,
                  s5_lam_im=s5_lam_im, s5_log_dt=s5_log_dt, s5_b_re=s5_b_re, s5_b_im=s5_b_im, s5_c_re=s5_c_re,
                  s5_c_im=s5_c_im, s5_d=s5_d, s5_w_glu=s5_w_glu, s5_b_glu=s5_b_glu, fox_b_f=fox_b_f,
                  w_mix_out=w_mix_out, ln1_g=ln1_g, ln1_b=ln1_b, mem_w_q=mem_w_q, mem_w_k=mem_w_k,
                  mem_w_v=mem_w_v, mem_w_o=mem_w_o, ln2_g=ln2_g, ln2_b=ln2_b, ffn_w_up=ffn_w_up,
                  ffn_conv_w=ffn_conv_w, ffn_conv_b=ffn_conv_b, ffn_w_down=ffn_w_down, ln3_g=ln3_g, ln3_b=ln3_b)
    bp, tp, _ = x_prompt.shape
    bs, ts, _ = x_sample.shape
    g_in, b_in = ln_in_g.reshape(1, D_MODEL), ln_in_b.reshape(1, D_MODEL)
    hp = _ln_call(x_prompt.reshape(bp * tp, D_MODEL), g_in, b_in).reshape(bp, tp, D_MODEL)
    hs = _ln_call(x_sample.reshape(bs * ts, D_MODEL), g_in, b_in).reshape(bs, ts, D_MODEL)
    mem2 = mem_prompt.reshape(bp * MEM_TOKENS, D_MODEL)
    zero_gla = jnp.zeros((bp, GLA_HEADS, GLA_DK, GLA_DV), F32)
    zero_s5 = jnp.zeros((bp, S5_GROUPS, S5_STATE), F32)
    zero_conv = jnp.zeros((bp, CONV_W - 1, 2 * D_FF), F32)
    prompt_states, sample_states = [], []
    for l in range(DEPTH):
        lp = _prep_layer(l, params)
        mk2, mv2 = _mem_kv_call(mem2, lp['wk'], lp['wv'])
        mk_p = mk2.reshape(bp, MEM_TOKENS, MEM_W)
        mv_p = mv2.reshape(bp, MEM_TOKENS, MEM_W)
        hp, st_p = _layer(hp, mk_p, mv_p, zero_gla, zero_s5, zero_s5, None, None, None, zero_conv, lp)
        prompt_states.append(st_p + (mk_p.reshape(bp, MEM_TOKENS, MEM_HEADS, MEM_HD),
                                     mv_p.reshape(bp, MEM_TOKENS, MEM_HEADS, MEM_HD)))
        hs, st_s = _layer(hs, cache_mem_k[l].reshape(bs, MEM_TOKENS, MEM_W),
                          cache_mem_v[l].reshape(bs, MEM_TOKENS, MEM_W), state_gla[l], state_s5_re[l],
                          state_s5_im[l], cache_fox_k[l], cache_fox_v[l], cache_fox_logf[l], state_ffn_conv[l], lp)
        sample_states.append(st_s)
    p_out = [jnp.stack(z) for z in zip(*prompt_states)]
    s_out = [jnp.stack(z) for z in zip(*sample_states)]
    return (hp, hs, *p_out, *s_out)
```
